```python
import math, functools
import jax, jax.numpy as jnp
from jax import lax
import numpy as np

D_MODEL = 1024
BATCH = 4
SEQ = 4096
DEPTH = 2
DEC_BATCH = 32
DEC_SEQ = 4
PAST_LEN = 8192
PAGE_SIZE = 128

BRANCH_WIDTH = D_MODEL // 2
N_BRANCHES = 3
GM_GROUPS = 4
GM_GDIM = BRANCH_WIDTH // GM_GROUPS
GM_CHUNK = 128
MOBA_HEADS = 8
MOBA_DH = BRANCH_WIDTH // MOBA_HEADS
MOBA_BLOCK = 256
MOBA_TOPK = 3
MOBA_Q_BLOCK = 32
MOBA_ROT = MOBA_DH // 4
DIFF_HEADS = 4
DIFF_DH = BRANCH_WIDTH // (2 * DIFF_HEADS)
DIFF_Q_BLOCK = 128
DIFF_ROT = DIFF_DH // 4
ROPE_THETA = 500000.0
N_GROUPS = 4
EXPERTS_PER_GROUP = 8
N_EXPERTS = N_GROUPS * EXPERTS_PER_GROUP
EXPERT_TOPK = 2
EXPERT_FF = D_MODEL // 2
MOE_BLOCK = 128
IN_SPLIT_SIZES = (BRANCH_WIDTH,) * 8 + (D_MODEL,) * N_BRANCHES
IN_COLS = 8 * BRANCH_WIDTH + N_BRANCHES * D_MODEL
EPS = 1e-6
NEG = -1e30

kernel_name = 'hybrid_gmlp_moba_diffattn_hmoe_step'


def rmsnorm(x, g):
    xf = x.astype(jnp.float32)
    y = xf * lax.rsqrt(jnp.mean(xf * xf, axis=-1, keepdims=True) + EPS)
    return (y * g.astype(jnp.float32)).astype(x.dtype)


def layernorm(x, g, b):
    xf = x.astype(jnp.float32)
    mu = jnp.mean(xf, axis=-1, keepdims=True)
    xc = xf - mu
    var = jnp.mean(xc * xc, axis=-1, keepdims=True)
    return (xc * lax.rsqrt(var + EPS) * g.astype(jnp.float32) + b.astype(jnp.float32)).astype(x.dtype)


def rope_partial(x, pos, rot):
    half = rot // 2
    inv = ROPE_THETA ** (-(jnp.arange(half, dtype=jnp.float32) * 2.0 / rot))
    ang = pos.astype(jnp.float32)[:, None] * inv[None, :]
    shape = (ang.shape[0],) + (1,) * (x.ndim - 3) + (half,)
    cos = jnp.cos(ang).reshape(shape)
    sin = jnp.sin(ang).reshape(shape)
    xf = x.astype(jnp.float32)
    x1 = xf[..., :half]
    x2 = xf[..., half:rot]
    out = jnp.concatenate([x1 * cos - x2 * sin, x2 * cos + x1 * sin, xf[..., rot:]], axis=-1)
    return out.astype(x.dtype)


def ada_mod(c, w_ada, b_ada):
    m = jax.nn.silu(c) @ w_ada + b_ada
    return jnp.split(m[:, None, :], 6, axis=-1)


def gmlp_spatial(u, v, ws, bs):
    b, s, w = v.shape
    n = -(-s // GM_CHUNK)
    vp = jnp.pad(v, ((0, 0), (0, n * GM_CHUNK - s), (0, 0))).reshape(b, n, GM_CHUNK, GM_GROUPS, GM_GDIM)
    mask = jnp.tril(jnp.ones((GM_CHUNK, GM_CHUNK), dtype=bool))
    wsm = jnp.where(mask[None], ws, 0)
    mixed = jnp.einsum('gts,bnsgd->bntgd', wsm, vp) + jnp.swapaxes(bs, 0, 1)[None, None, :, :, None]
    mixed = mixed.reshape(b, n * GM_CHUNK, w)[:, :s]
    return u * mixed


def key_blocks(k):
    b, L, h, dh = k.shape
    nb = -(-L // MOBA_BLOCK)
    k = jnp.pad(k, ((0, 0), (0, nb * MOBA_BLOCK - L), (0, 0), (0, 0)))
    return jnp.transpose(k.reshape(b, nb, MOBA_BLOCK, h, dh), (0, 3, 1, 2, 4))


def moba_core(q, qpos, kbh, vbh, kmean):
    b, nq, h, dh = q.shape
    nb = kbh.shape[2]
    k_sel = min(MOBA_TOPK, nb)
    qh = jnp.swapaxes(q, 1, 2)
    gate = jnp.einsum('bhqd,bhnd->bhqn', qh, kmean).astype(jnp.float32)
    own = qpos // MOBA_BLOCK
    past_ok = jnp.arange(nb)[None, :] < own[:, None]
    gate = jnp.where(past_ok[None, None], gate, NEG)
    _, idx = lax.top_k(gate, k_sel)
    valid = jnp.arange(k_sel)[None, :] < own[:, None]
    bi = jnp.arange(b)[:, None, None, None]
    hi = jnp.arange(h)[None, :, None, None]
    sel_k = kbh[bi, hi, idx]
    sel_v = vbh[bi, hi, idx]
    own_k = kbh[:, :, own]
    own_v = vbh[:, :, own]
    scale = dh ** -0.5
    s_past = jnp.einsum('bhqd,bhqnkd->bhqnk', qh, sel_k).astype(jnp.float32) * scale
    s_past = jnp.where(valid[None, None, :, :, None], s_past, NEG)
    kpos = own[:, None] * MOBA_BLOCK + jnp.arange(MOBA_BLOCK)[None, :]
    s_own = jnp.einsum('bhqd,bhqkd->bhqk', qh, own_k).astype(jnp.float32) * scale
    s_own = jnp.where((kpos <= qpos[:, None])[None, None], s_own, NEG)
    n_past = k_sel * MOBA_BLOCK
    p = jax.nn.softmax(jnp.concatenate([s_past.reshape(b, h, nq, n_past), s_own], axis=-1), axis=-1).astype(q.dtype)
    p_past = p[..., :n_past].reshape(b, h, nq, k_sel, MOBA_BLOCK)
    o = jnp.einsum('bhqnk,bhqnkd->bhqd', p_past, sel_v) + jnp.einsum('bhqk,bhqkd->bhqd', p[..., n_past:], own_v)
    return jnp.swapaxes(o, 1, 2)


def diff_core(q, qpos, k, v, lam):
    s = jnp.einsum('bqhcd,bkhcd->bhcqk', q, k).astype(jnp.float32) * (DIFF_DH ** -0.5)
    mask = jnp.arange(k.shape[1])[None, :] <= qpos[:, None]
    p = jax.nn.softmax(jnp.where(mask[None, None, None], s, NEG), axis=-1)
    a = (p[:, :, 0] - lam * p[:, :, 1]).astype(v.dtype)
    return jnp.einsum('bhqk,bkhe->bqhe', a, v)


def blocked_queries(fn, q, qpos, qb):
    b, s = q.shape[:2]
    n = s // qb
    qs = jnp.moveaxis(q.reshape((b, n, qb) + q.shape[2:]), 1, 0)
    o = lax.map(lambda a: fn(a[0], a[1]), (qs, qpos.reshape(n, qb)))
    return jnp.moveaxis(o, 0, 1).reshape((b, s) + o.shape[3:])


def gather_pages(cache_l, page_table):
    g = cache_l[page_table]
    return g.reshape((g.shape[0], g.shape[1] * g.shape[2]) + g.shape[3:])


def attend_prompt(mq, mk, mv, dq, dk, dv, pos, lam):
    kbh = key_blocks(mk)
    vbh = key_blocks(mv)
    kmean = jnp.mean(kbh.astype(jnp.float32), axis=3).astype(mk.dtype)
    o_b = blocked_queries(lambda q, p: moba_core(q, p, kbh, vbh, kmean), mq, pos, MOBA_Q_BLOCK)
    o_c = blocked_queries(lambda q, p: diff_core(q, p, dk, dv, lam), dq, pos, DIFF_Q_BLOCK)
    return o_b, o_c


def attend_sample(mq, mk, mv, dq, dk, dv, pos, lam, ck, cv, cdk, cdv, page_table):
    k_all = jnp.concatenate([gather_pages(ck, page_table).astype(mk.dtype), mk], axis=1)
    v_all = jnp.concatenate([gather_pages(cv, page_table).astype(mv.dtype), mv], axis=1)
    kbh = key_blocks(k_all)
    vbh = key_blocks(v_all)
    kmean = jnp.mean(kbh.astype(jnp.float32), axis=3).astype(mk.dtype)
    o_b = moba_core(mq, pos, kbh, vbh, kmean)
    dk_all = jnp.concatenate([gather_pages(cdk, page_table).astype(dk.dtype), dk], axis=1)
    dv_all = jnp.concatenate([gather_pages(cdv, page_table).astype(dv.dtype), dv], axis=1)
    o_c = diff_core(dq, pos, dk_all, dv_all, lam)
    return o_b, o_c


def expert_mlp(x, wg, wu, wd):
    return (jax.nn.silu(x @ wg) * (x @ wu)) @ wd


def grouped_experts(xt, expert_id, weight, wg, wu, wd):
    T, d = xt.shape
    A = T * EXPERT_TOPK
    flat_e = expert_id.reshape(-1)
    order = jnp.argsort(flat_e)
    sorted_e = flat_e[order]
    counts = jnp.zeros((N_EXPERTS,), jnp.int32).at[flat_e].add(1)
    starts = jnp.cumsum(counts) - counts
    padded = (counts + MOE_BLOCK - 1) // MOE_BLOCK * MOE_BLOCK
    pad_ends = jnp.cumsum(padded)
    pad_starts = pad_ends - padded
    dest = pad_starts[sorted_e] + jnp.arange(A, dtype=jnp.int32) - starts[sorted_e]
    n_blocks = -(-A // MOE_BLOCK) + N_EXPERTS
    src_tok = jnp.full((n_blocks * MOE_BLOCK,), T, jnp.int32).at[dest].set((order // EXPERT_TOPK).astype(jnp.int32))
    x_pad = jnp.concatenate([xt, jnp.zeros((1, d), xt.dtype)], axis=0)
    xb = x_pad[src_tok].reshape(n_blocks, MOE_BLOCK, d)
    blk_start = jnp.arange(n_blocks, dtype=jnp.int32) * MOE_BLOCK
    blk_e = jnp.minimum(jnp.searchsorted(pad_ends, blk_start, side='right'), N_EXPERTS - 1)
    yb = lax.map(lambda a: expert_mlp(a[1], wg[a[0]], wu[a[0]], wd[a[0]]), (blk_e, xb)).reshape(-1, d)
    y_assign = jnp.zeros((A, d), yb.dtype).at[order].set(yb[dest])
    return jnp.einsum('tkd,tk->td', y_assign.reshape(T, EXPERT_TOPK, d), weight.astype(yb.dtype))


def hier_moe(x, w_group, b_group, w_router, b_router, wg, wu, wd):
    b, s, d = x.shape
    xt = x.reshape(-1, d)
    T = xt.shape[0]
    tok = jnp.arange(T)
    g_logits = (xt @ w_group).astype(jnp.float32) + b_group.astype(jnp.float32)
    g_prob = jax.nn.softmax(g_logits, axis=-1)
    g_sel = jnp.argmax(g_logits, axis=-1)
    g_w = g_prob[tok, g_sel]
    e_logits = ((xt @ w_router).astype(jnp.float32) + b_router.astype(jnp.float32)).reshape(T, N_GROUPS, EXPERTS_PER_GROUP)
    e_logits = e_logits[tok, g_sel]
    top_l, top_i = lax.top_k(e_logits, EXPERT_TOPK)
    e_w = jax.nn.softmax(top_l, axis=-1) * g_w[:, None]
    expert_id = (g_sel[:, None] * EXPERTS_PER_GROUP + top_i).astype(jnp.int32)
    return grouped_experts(xt, expert_id, e_w, wg, wu, wd).reshape(b, s, d)


def trunk_layer(x, c, pos, lam_init, attend, n1, n2, wada, bada, win, lng, lnb, ws, bs, dlam, subg,
                wbr, wo, wgrp, bgrp, wrt, brt, weg, weu, wed):
    b, s, _ = x.shape
    sh1, sc1, gt1, sh2, sc2, gt2 = ada_mod(c, wada, bada)
    h = rmsnorm(x, n1) * (1 + sc1) + sh1
    z = h @ win
    split_at = np.cumsum(IN_SPLIT_SIZES)[:-1].tolist()
    gu, gv, mq, mk, mv, dq, dk, dv, ga, gb, gc = jnp.split(z, split_at, axis=-1)
    u = jax.nn.gelu(gu)
    v_gm = layernorm(jax.nn.gelu(gv), lng, lnb)
    o_a = gmlp_spatial(u, v_gm, ws, bs)
    mq = rope_partial(mq.reshape(b, s, MOBA_HEADS, MOBA_DH), pos, MOBA_ROT)
    mk = rope_partial(mk.reshape(b, s, MOBA_HEADS, MOBA_DH), pos, MOBA_ROT)
    mv = mv.reshape(b, s, MOBA_HEADS, MOBA_DH)
    dq = rope_partial(dq.reshape(b, s, DIFF_HEADS, 2, DIFF_DH), pos, DIFF_ROT)
    dk = rope_partial(dk.reshape(b, s, DIFF_HEADS, 2, DIFF_DH), pos, DIFF_ROT)
    dv = dv.reshape(b, s, DIFF_HEADS, 2 * DIFF_DH)
    dl = dlam.astype(jnp.float32)
    lam = jnp.exp(jnp.sum(dl[0] * dl[1])) - jnp.exp(jnp.sum(dl[2] * dl[3])) + lam_init
    o_b, o_c = attend(mq, mk, mv, dq, dk, dv, pos, lam)
    o_c = rmsnorm(o_c, subg) * (1.0 - lam_init)
    merged = (jax.nn.sigmoid(ga) * (o_a @ wbr[0])
              + jax.nn.sigmoid(gb) * (o_b.reshape(b, s, BRANCH_WIDTH) @ wbr[1])
              + jax.nn.sigmoid(gc) * (o_c.reshape(b, s, BRANCH_WIDTH) @ wbr[2]))
    x = x + gt1 * (merged @ wo)
    h2 = rmsnorm(x, n2) * (1 + sc2) + sh2
    x = x + gt2 * hier_moe(h2, wgrp, bgrp, wrt, brt, weg, weu, wed)
    return x, (mk, mv, dk, dv, v_gm)


def setup_inputs(seed: int = 0) -> dict:
    key = jax.random.key(seed)
    ks = jax.random.split(key, 32)
    f32 = jnp.float32
    nrm = lambda k, shape, s: jax.random.normal(k, shape, f32) * s
    n_pages = PAST_LEN // PAGE_SIZE
    n_used = DEC_BATCH * n_pages
    n_pool = n_used + max(1, n_used // 4)
    page_table = jax.random.permutation(ks[0], n_pool)[:n_used].reshape(DEC_BATCH, n_pages).astype(jnp.int32)
    D = D_MODEL
    return {
        'x_prompt': nrm(ks[1], (BATCH, SEQ, D), 1.0),
        'x_sample': nrm(ks[2], (DEC_BATCH, DEC_SEQ, D), 1.0),
        'cache_moba_k': nrm(ks[3], (DEPTH, n_pool, PAGE_SIZE, MOBA_HEADS, MOBA_DH), 1.0),
        'cache_moba_v': nrm(ks[4], (DEPTH, n_pool, PAGE_SIZE, MOBA_HEADS, MOBA_DH), 1.0),
        'cache_diff_k': nrm(ks[5], (DEPTH, n_pool, PAGE_SIZE, DIFF_HEADS, 2, DIFF_DH), 1.0),
        'cache_diff_v': nrm(ks[6], (DEPTH, n_pool, PAGE_SIZE, DIFF_HEADS, 2 * DIFF_DH), 1.0),
        'page_table': page_table,
        'c_prompt': nrm(ks[7], (BATCH, D), 1.0),
        'c_sample': nrm(ks[8], (DEC_BATCH, D), 1.0),
        'norm1_g': 1.0 + nrm(ks[9], (DEPTH, D), 0.02),
        'norm2_g': 1.0 + nrm(ks[10], (DEPTH, D), 0.02),
        'final_g': 1.0 + nrm(ks[11], (D,), 0.02),
        'w_ada': nrm(ks[12], (DEPTH, D, 6 * D), 0.5 * D ** -0.5),
        'b_ada': nrm(ks[13], (DEPTH, 6 * D), 0.02),
        'w_in': nrm(ks[14], (DEPTH, D, IN_COLS), D ** -0.5),
        'gm_ln_g': 1.0 + nrm(ks[15], (DEPTH, BRANCH_WIDTH), 0.02),
        'gm_ln_b': nrm(ks[16], (DEPTH, BRANCH_WIDTH), 0.02),
        'gm_ws': nrm(ks[17], (DEPTH, GM_GROUPS, GM_CHUNK, GM_CHUNK), GM_CHUNK ** -0.5),
        'gm_bs': 1.0 + nrm(ks[18], (DEPTH, GM_GROUPS, GM_CHUNK), 0.02),
        'diff_lambda': nrm(ks[19], (DEPTH, 4, DIFF_DH), 0.1),
        'diff_subln_g': 1.0 + nrm(ks[20], (DEPTH, 2 * DIFF_DH), 0.02),
        'w_branch': nrm(ks[21], (DEPTH, N_BRANCHES, BRANCH_WIDTH, D), BRANCH_WIDTH ** -0.5),
        'w_out': nrm(ks[22], (DEPTH, D, D), D ** -0.5),
        'w_group': nrm(ks[23], (DEPTH, D, N_GROUPS), D ** -0.5),
        'b_group': nrm(ks[24], (DEPTH, N_GROUPS), 0.01),
        'w_router': nrm(ks[25], (DEPTH, D, N_EXPERTS), D ** -0.5),
        'b_router': nrm(ks[26], (DEPTH, N_EXPERTS), 0.01),
        'w_gate_e': nrm(ks[27], (DEPTH, N_EXPERTS, D, EXPERT_FF), D ** -0.5),
        'w_up_e': nrm(ks[28], (DEPTH, N_EXPERTS, D, EXPERT_FF), D ** -0.5),
        'w_down_e': nrm(ks[29], (DEPTH, N_EXPERTS, EXPERT_FF, D), EXPERT_FF ** -0.5),
    }


def reference(x_prompt, x_sample, cache_moba_k, cache_moba_v, cache_diff_k, cache_diff_v, page_table,
              c_prompt, c_sample, norm1_g, norm2_g, final_g, w_ada, b_ada, w_in, gm_ln_g, gm_ln_b,
              gm_ws, gm_bs, diff_lambda, diff_subln_g, w_branch, w_out, w_group, b_group, w_router,
              b_router, w_gate_e, w_up_e, w_down_e):
    pos_p = jnp.arange(x_prompt.shape[1], dtype=jnp.int32)
    pos_s = PAST_LEN + jnp.arange(x_sample.shape[1], dtype=jnp.int32)
    xp, xs = x_prompt, x_sample
    mk_p, mv_p, dk_p, dv_p = [], [], [], []
    mk_s, mv_s, dk_s, dv_s, gv_s = [], [], [], [], []
    for l in range(DEPTH):
        lam_init = 0.8 - 0.6 * math.exp(-0.3 * l)
        wl = (norm1_g[l], norm2_g[l], w_ada[l], b_ada[l], w_in[l], gm_ln_g[l], gm_ln_b[l], gm_ws[l], gm_bs[l],
              diff_lambda[l], diff_subln_g[l], w_branch[l], w_out[l], w_group[l], b_group[l], w_router[l],
              b_router[l], w_gate_e[l], w_up_e[l], w_down_e[l])
        xp, (a, b_, c_, d_, _) = trunk_layer(xp, c_prompt, pos_p, lam_init, attend_prompt, *wl)
        mk_p.append(a); mv_p.append(b_); dk_p.append(c_); dv_p.append(d_)
        att_s = functools.partial(attend_sample, ck=cache_moba_k[l], cv=cache_moba_v[l], cdk=cache_diff_k[l],
                                  cdv=cache_diff_v[l], page_table=page_table)
        xs, (a, b_, c_, d_, e_) = trunk_layer(xs, c_sample, pos_s, lam_init, att_s, *wl)
        mk_s.append(a); mv_s.append(b_); dk_s.append(c_); dv_s.append(d_); gv_s.append(e_)
    y_prompt = rmsnorm(xp, final_g)
    y_sample = rmsnorm(xs, final_g)
    return (y_prompt, y_sample,
            jnp.stack(mk_p), jnp.stack(mv_p), jnp.stack(dk_p), jnp.stack(dv_p),
            jnp.stack(mk_s), jnp.stack(mv_s), jnp.stack(dk_s), jnp.stack(dv_s), jnp.stack(gv_s))
```

```python
import functools
import math

import jax
import jax.numpy as jnp
import numpy as np
from jax import lax
from jax.experimental import pallas as pl
from jax.experimental.pallas import tpu as pltpu

F32 = jnp.float32
BF16 = jnp.bfloat16
HIGHEST = lax.Precision.HIGHEST

EPS = 1e-6
NEG = -1e30
BIG = 1e30

N_BRANCHES = 3
GM_GROUPS = 4
GM_CHUNK = 128
N_UNITS = 8
HEAD_DIM = 64
ROT = HEAD_DIM // 4
ROPE_THETA = 500000.0
ATT_BLOCK = 256
MOBA_TOPK = 3
N_GROUPS = 4
EXPERTS_PER_GROUP = 8
N_EXPERTS = N_GROUPS * EXPERTS_PER_GROUP
EXPERT_TOPK = 2
MOE_ROWS = 256
LANES = 128
VMEM_LIMIT = 48 * 1024 * 1024


def _cparams(sem):
    return pltpu.CompilerParams(dimension_semantics=sem, vmem_limit_bytes=VMEM_LIMIT)


def _ada_kernel(c_ref, w_ref, b_ref, o_ref):
    c = c_ref[...]
    a = c * jax.nn.sigmoid(c)
    o_ref[...] = jnp.dot(a, w_ref[...], preferred_element_type=F32, precision=HIGHEST) + b_ref[...]


def ada_call(c_all, w_ada, b_ada, layer):
    m, d = c_all.shape
    n = w_ada.shape[-1]
    tn = n // 4
    return pl.pallas_call(
        _ada_kernel,
        grid=(n // tn,),
        in_specs=[
            pl.BlockSpec((m, d), lambda j: (0, 0)),
            pl.BlockSpec((None, d, tn), lambda j: (layer, 0, j)),
            pl.BlockSpec((None, 1, tn), lambda j: (layer, 0, j)),
        ],
        out_specs=pl.BlockSpec((m, tn), lambda j: (0, j)),
        out_shape=jax.ShapeDtypeStruct((m, n), F32),
        compiler_params=_cparams(("arbitrary",)),
        name="ada",
    )(c_all, w_ada, b_ada.reshape(b_ada.shape[0], 1, n))


def _inproj_kernel(x_ref, sc_ref, sh_ref, g_ref, w_ref, rc_ref, rs1_ref, rs2_ref, lng_ref, lnb_ref,
                   o_ref, h_scr):
    j = pl.program_id(1)

    @pl.when(j == 0)
    def _():
        x = x_ref[...]
        y = x * lax.rsqrt(jnp.mean(x * x, axis=-1, keepdims=True) + EPS) * g_ref[...]
        h_scr[...] = (y * (1.0 + sc_ref[0]) + sh_ref[0]).astype(BF16)

    z = jnp.dot(h_scr[...], w_ref[...], preferred_element_type=F32)
    width = z.shape[-1]
    reps = width // LANES

    @pl.when(j == 0)
    def _():
        o_ref[...] = jax.nn.gelu(z)

    @pl.when(j == 1)
    def _():
        a = jax.nn.gelu(z)
        mu = jnp.mean(a, axis=-1, keepdims=True)
        ac = a - mu
        var = jnp.mean(ac * ac, axis=-1, keepdims=True)
        o_ref[...] = ac * lax.rsqrt(var + EPS) * lng_ref[...] + lnb_ref[...]

    @pl.when((j == 2) | (j == 3) | (j == 5) | (j == 6))
    def _():
        c = jnp.concatenate([rc_ref[...]] * reps, axis=1)
        s1 = jnp.concatenate([rs1_ref[...]] * reps, axis=1)
        s2 = jnp.concatenate([rs2_ref[...]] * reps, axis=1)
        up = pltpu.roll(z, width - ROT // 2, 1)
        dn = pltpu.roll(z, ROT // 2, 1)
        o_ref[...] = z * c + up * s1 + dn * s2

    @pl.when((j == 4) | (j == 7))
    def _():
        o_ref[...] = z

    @pl.when(j >= 8)
    def _():
        o_ref[...] = jax.nn.sigmoid(z)


def inproj_call(x, sc, sh, g, w_bf, rope, lng, lnb, tm, seq_tiles):
    t, d = x.shape
    n = w_bf.shape[1]
    tn = 512
    mod_rows = sc.shape[1]
    if mod_rows == 1:
        mod_map = lambda i, j: (i // seq_tiles, 0, 0)
    else:
        mod_map = lambda i, j: (0, 0, 0)
    rope_map = lambda i, j: (i % seq_tiles, 0)
    rc, rs1, rs2 = rope
    return pl.pallas_call(
        _inproj_kernel,
        grid=(t // tm, n // tn),
        in_specs=[
            pl.BlockSpec((tm, d), lambda i, j: (i, 0)),
            pl.BlockSpec((1, mod_rows, d), mod_map),
            pl.BlockSpec((1, mod_rows, d), mod_map),
            pl.BlockSpec((1, d), lambda i, j: (0, 0)),
            pl.BlockSpec((d, tn), lambda i, j: (0, j)),
            pl.BlockSpec((tm, LANES), rope_map),
            pl.BlockSpec((tm, LANES), rope_map),
            pl.BlockSpec((tm, LANES), rope_map),
            pl.BlockSpec((1, tn), lambda i, j: (0, 0)),
            pl.BlockSpec((1, tn), lambda i, j: (0, 0)),
        ],
        out_specs=pl.BlockSpec((tm, tn), lambda i, j: (i, j)),
        out_shape=jax.ShapeDtypeStruct((t, n), F32),
        scratch_shapes=[pltpu.VMEM((tm, d), BF16)],
        compiler_params=_cparams(("arbitrary", "arbitrary")),
        name="inproj",
    )(x, sc, sh, g, w_bf, rc, rs1, rs2, lng, lnb)


def rope_tables(pos):
    half = ROT // 2
    inv = ROPE_THETA ** (-(jnp.arange(half, dtype=F32) * 2.0 / ROT))
    ang = pos.astype(F32)[:, None] * inv[None, :]
    cos, sin = jnp.cos(ang), jnp.sin(ang)
    ones = jnp.ones((pos.shape[0], HEAD_DIM - ROT), F32)
    zeros8 = jnp.zeros((pos.shape[0], half), F32)
    zrest = jnp.zeros_like(ones)
    c = jnp.concatenate([cos, cos, ones], axis=1)
    s1 = jnp.concatenate([-sin, zeros8, zrest], axis=1)
    s2 = jnp.concatenate([zeros8, sin, zrest], axis=1)
    tile = lambda a: jnp.concatenate([a] * (LANES // HEAD_DIM), axis=1)
    return tile(c), tile(s1), tile(s2)


def _attn_kernel(*refs, nmaps, use_sel, lam_init):
    if use_sel:
        q_ref, k_ref, v_ref, o_ref, kmean_scr, sel_scr = refs
    else:
        q_ref, k_ref, v_ref, dl_ref, sg_ref, o_ref = refs
    qi = pl.program_id(2)
    nb = k_ref.shape[1]
    tq = q_ref.shape[-1]
    blk = k_ref.shape[2]

    if use_sel:
        @pl.when(qi == 0)
        def _():
            kmean_scr[...] = jnp.mean(k_ref[0].astype(F32), axis=1)

        gate = jnp.dot(kmean_scr[...], q_ref[0].astype(F32), preferred_element_type=F32,
                       precision=HIGHEST)
        row = lax.broadcasted_iota(jnp.int32, (nb, tq), 0)
        gate = jnp.where(row < qi, gate, NEG)
        sel = jnp.zeros((nb, tq), F32)
        for j in range(nb):
            gj = gate[j:j + 1, :]
            beats = jnp.where(gate > gj, 1.0, jnp.where((gate == gj) & (row < j), 1.0, 0.0))
            cnt = jnp.sum(beats, axis=0, keepdims=True)
            sj = jnp.where(cnt < MOBA_TOPK, 1.0, 0.0)
            sel = jnp.where((row == j) & (row < qi), sj, sel)
        sel_scr[...] = sel

    kr = lax.broadcasted_iota(jnp.int32, (blk, tq), 0)
    qc = lax.broadcasted_iota(jnp.int32, (blk, tq), 1)
    outs = []
    for c in range(nmaps):
        qt = q_ref[c]
        s = jnp.dot(k_ref[c, qi], qt, preferred_element_type=F32)
        s = jnp.where(kr <= qc, s, NEG)
        m0 = jnp.max(s, axis=0, keepdims=True)
        p = jnp.exp(s - m0)
        l0 = jnp.sum(p, axis=0, keepdims=True)
        a0 = jnp.dot(v_ref[qi], p.astype(BF16), preferred_element_type=F32)

        def kv_body(kj, carry, c=c, qt=qt):
            m, l, acc = carry
            s = jnp.dot(k_ref[c, kj], qt, preferred_element_type=F32)
            smax = jnp.max(s, axis=0, keepdims=True)
            if use_sel:
                on = sel_scr[pl.ds(kj, 1), :] > 0.5
                m_new = jnp.where(on, jnp.maximum(m, smax), m)
                m_use = jnp.where(on, m_new, BIG)
            else:
                m_new = jnp.maximum(m, smax)
                m_use = m_new
            p = jnp.exp(s - m_use)
            alpha = jnp.exp(m - m_new)
            l = alpha * l + jnp.sum(p, axis=0, keepdims=True)
            acc = alpha * acc + jnp.dot(v_ref[kj], p.astype(BF16), preferred_element_type=F32)
            return m_new, l, acc

        _, l, acc = lax.fori_loop(0, qi, kv_body, (m0, l0, a0))
        outs.append(acc / l)

    if use_sel:
        o_ref[...] = outs[0]
    else:
        dl = dl_ref[...]
        lam = (jnp.exp(jnp.sum(dl[0:1] * dl[1:2], axis=1, keepdims=True))
               - jnp.exp(jnp.sum(dl[2:3] * dl[3:4], axis=1, keepdims=True)) + lam_init)
        o = outs[0] - lam * outs[1]
        ms = jnp.mean(o * o, axis=0, keepdims=True)
        o_ref[...] = o * lax.rsqrt(ms + EPS) * sg_ref[...] * (1.0 - lam_init)


def attn_call(qt, k, vt, nmaps, use_sel, dlam=None, subg=None, lam_init=0.0):
    b, nu, dh, s = qt.shape
    nb, blk = k.shape[2], k.shape[3]
    nv, dv = vt.shape[1], vt.shape[3]
    tq = blk
    kern = functools.partial(_attn_kernel, nmaps=nmaps, use_sel=use_sel, lam_init=lam_init)
    in_specs = [
        pl.BlockSpec((None, nmaps, dh, tq), lambda bi, n, qi: (bi, n, 0, qi)),
        pl.BlockSpec((None, nmaps, nb, blk, dh), lambda bi, n, qi: (bi, n, 0, 0, 0)),
        pl.BlockSpec((None, None, nb, dv, blk), lambda bi, n, qi: (bi, n, 0, 0, 0)),
    ]
    args = [qt, k, vt]
    scratch = []
    if use_sel:
        nbp = max(8, nb)
        scratch = [pltpu.VMEM((nb, dh), F32), pltpu.VMEM((nb, tq), F32)]
        del nbp
    else:
        in_specs += [pl.BlockSpec(dlam.shape, lambda bi, n, qi: (0, 0)),
                     pl.BlockSpec(subg.shape, lambda bi, n, qi: (0, 0))]
        args += [dlam, subg]
    return pl.pallas_call(
        kern,
        grid=(b, nv, s // tq),
        in_specs=in_specs,
        out_specs=pl.BlockSpec((None, None, dv, tq), lambda bi, n, qi: (bi, n, 0, qi)),
        out_shape=jax.ShapeDtypeStruct((b, nv, dv, s), F32),
        scratch_shapes=scratch,
        compiler_params=_cparams(("arbitrary", "arbitrary", "arbitrary")),
        name="moba_attn" if use_sel else "diff_attn",
    )(*args)


def _sattn_kernel(pt_ref, qbd_ref, k0_ref, k1_ref, v0_ref, v1_ref, kn_ref, vn_ref, om_ref, dl_ref,
                  sg_ref, o_ref, opart, g_scr, m_scr, l_scr, *, use_sel, nblk, n_new, lam_init):
    del pt_ref
    j = pl.program_id(1)
    qf = qbd_ref[0]
    qb = qf.astype(BF16)
    rows = qf.shape[0]
    kblk = jnp.concatenate([k0_ref[...], k1_ref[...]], axis=0)
    vblk = jnp.concatenate([v0_ref[...], v1_ref[...]], axis=0)
    s = lax.dot_general(qb, kblk.astype(BF16), (((1,), (1,)), ((), ())), preferred_element_type=F32)
    mj = jnp.max(s, axis=-1, keepdims=True)
    p = jnp.exp(s - mj)
    lj = jnp.sum(p, axis=-1, keepdims=True)
    oj = jnp.dot(p.astype(BF16), vblk.astype(BF16), preferred_element_type=F32)
    opart[j] = oj * om_ref[...]
    lane = lax.broadcasted_iota(jnp.int32, (rows, LANES), 1)

    @pl.when(j == 0)
    def _():
        g_scr[...] = jnp.full((rows, LANES), NEG, F32)
        m_scr[...] = jnp.full((rows, LANES), NEG, F32)
        l_scr[...] = jnp.zeros((rows, LANES), F32)

    hit = lane == j
    if use_sel:
        kmean = jnp.mean(kblk, axis=0, keepdims=True)
        gj = jnp.sum(qf * kmean, axis=-1, keepdims=True)
        g_scr[...] = jnp.where(hit, gj, g_scr[...])
    m_scr[...] = jnp.where(hit, mj, m_scr[...])
    l_scr[...] = jnp.where(hit, lj, l_scr[...])

    @pl.when(j == nblk - 1)
    def _():
        kn = kn_ref[0].astype(BF16)
        vn = vn_ref[0].astype(BF16)
        so = lax.dot_general(qb, kn, (((1,), (1,)), ((), ())), preferred_element_type=F32)
        rtok = lax.broadcasted_iota(jnp.int32, (rows, LANES), 0) // N_UNITS
        so = jnp.where((lane <= rtok) & (lane < n_new), so, NEG)
        mo = jnp.max(so, axis=-1, keepdims=True)
        po = jnp.exp(so - mo)
        lo = jnp.sum(po, axis=-1, keepdims=True)
        oo = jnp.dot(po.astype(BF16), vn, preferred_element_type=F32) * om_ref[...]

        inb = lane < nblk
        if use_sel:
            g = g_scr[...]
            selm = jnp.zeros((rows, LANES), F32)
            for jb in range(nblk):
                gj = g[:, jb:jb + 1]
                beats = jnp.where(inb & (g > gj), 1.0, jnp.where(inb & (g == gj) & (lane < jb), 1.0, 0.0))
                cnt = jnp.sum(beats, axis=-1, keepdims=True)
                selm = jnp.where(lane == jb, jnp.where(cnt < MOBA_TOPK, 1.0, 0.0), selm)
            selb = selm > 0.5
        else:
            selb = inb
        mm = m_scr[...]
        mt = jnp.maximum(jnp.max(jnp.where(selb, mm, NEG), axis=-1, keepdims=True), mo)
        w = jnp.where(selb, jnp.exp(mm - mt), 0.0)
        wo = jnp.exp(mo - mt)
        ltot = jnp.sum(w * l_scr[...], axis=-1, keepdims=True) + wo * lo
        acc = wo * oo
        for jb in range(nblk):
            acc = acc + w[:, jb:jb + 1] * opart[jb]
        res = acc / ltot
        if not use_sel:
            dl = dl_ref[...]
            lam = (jnp.exp(jnp.sum(dl[0:1] * dl[1:2], axis=1, keepdims=True))
                   - jnp.exp(jnp.sum(dl[2:3] * dl[3:4], axis=1, keepdims=True)) + lam_init)
            rr = lax.broadcasted_iota(jnp.int32, (rows, 1), 0)
            res = res * jnp.where(rr % 2 == 0, 1.0, -lam)
        wdt = res.shape[-1]
        out = jnp.sum(res.reshape(rows // N_UNITS, N_UNITS, wdt), axis=1)
        if not use_sel:
            dv = sg_ref.shape[-1]
            segs = []
            for hh in range(wdt // dv):
                seg = out[:, hh * dv:(hh + 1) * dv]
                ms = jnp.mean(seg * seg, axis=-1, keepdims=True)
                segs.append(seg * lax.rsqrt(ms + EPS) * sg_ref[...] * (1.0 - lam_init))
            out = jnp.concatenate(segs, axis=1)
        o_ref[0] = out


def sattn_call(page_table, qbd, cache_k, cache_v, knew, vnew, omask, dlam, subg, layer, use_sel,
               lam_init, n_new):
    db, rows, w = qbd.shape
    page = cache_k.shape[2]
    n_pages = page_table.shape[1]
    nblk = n_pages * page // ATT_BLOCK
    ppb = ATT_BLOCK // page
    assert ppb == 2 and nblk <= LANES
    pt_flat = page_table.reshape(-1)

    def page_map(which):
        return lambda b, j, pt: (layer, pt[b * n_pages + ppb * j + which], 0, 0)

    kern = functools.partial(_sattn_kernel, use_sel=use_sel, nblk=nblk, n_new=n_new, lam_init=lam_init)
    n_tok = rows // N_UNITS
    grid_spec = pltpu.PrefetchScalarGridSpec(
        num_scalar_prefetch=1,
        grid=(db, nblk),
        in_specs=[
            pl.BlockSpec((1, rows, w), lambda b, j, pt: (b, 0, 0)),
            pl.BlockSpec((None, None, page, w), page_map(0)),
            pl.BlockSpec((None, None, page, w), page_map(1)),
            pl.BlockSpec((None, None, page, w), page_map(0)),
            pl.BlockSpec((None, None, page, w), page_map(1)),
            pl.BlockSpec((1, LANES, w), lambda b, j, pt: (b, 0, 0)),
            pl.BlockSpec((1, LANES, w), lambda b, j, pt: (b, 0, 0)),
            pl.BlockSpec((rows, w), lambda b, j, pt: (0, 0)),
            pl.BlockSpec(dlam.shape, lambda b, j, pt: (0, 0)),
            pl.BlockSpec(subg.shape, lambda b, j, pt: (0, 0)),
        ],
        out_specs=pl.BlockSpec((1, n_tok, w), lambda b, j, pt: (b, 0, 0)),
        scratch_shapes=[pltpu.VMEM((nblk, rows, w), F32), pltpu.VMEM((rows, LANES), F32),
                        pltpu.VMEM((rows, LANES), F32), pltpu.VMEM((rows, LANES), F32)],
    )
    return pl.pallas_call(
        kern,
        grid_spec=grid_spec,
        out_shape=jax.ShapeDtypeStruct((db, n_tok, w), F32),
        compiler_params=_cparams(("arbitrary", "arbitrary")),
        name="moba_sample" if use_sel else "diff_sample",
    )(pt_flat, qbd, cache_k, cache_k, cache_v, cache_v, knew, vnew, omask, dlam, subg)


def _merge_kernel(x_ref, u_ref, v_ref, ob_ref, oc_ref, ga_ref, gb_ref, gc_ref, gt1_ref, sc2_ref,
                  sh2_ref, n2_ref, ws_ref, bs_ref, wbr_ref, wo_ref, wr_ref, br_ref,
                  x1_ref, h2_ref, rt_ref):
    tm = x_ref.shape[0]
    gd = ws_ref.shape[-1]
    v = v_ref[...].astype(BF16)
    chunks = []
    for c in range(tm // GM_CHUNK):
        r0 = c * GM_CHUNK
        cols = [jnp.dot(ws_ref[g], v[r0:r0 + GM_CHUNK, g * gd:(g + 1) * gd], preferred_element_type=F32)
                for g in range(GM_GROUPS)]
        chunks.append(jnp.concatenate(cols, axis=1) + bs_ref[...])
    mixed = jnp.concatenate(chunks, axis=0) if len(chunks) > 1 else chunks[0]
    o_a = u_ref[...] * mixed
    merged = (ga_ref[...] * jnp.dot(o_a.astype(BF16), wbr_ref[0], preferred_element_type=F32)
              + gb_ref[...] * jnp.dot(ob_ref[...].astype(BF16), wbr_ref[1], preferred_element_type=F32)
              + gc_ref[...] * jnp.dot(oc_ref[...].astype(BF16), wbr_ref[2], preferred_element_type=F32))
    y = jnp.dot(merged.astype(BF16), wo_ref[...], preferred_element_type=F32)
    x1 = x_ref[...] + gt1_ref[0] * y
    x1_ref[...] = x1
    h2 = x1 * lax.rsqrt(jnp.mean(x1 * x1, axis=-1, keepdims=True) + EPS) * n2_ref[...]
    h2 = h2 * (1.0 + sc2_ref[0]) + sh2_ref[0]
    h2_ref[...] = h2

    logits = jnp.dot(h2, wr_ref[...], preferred_element_type=F32, precision=HIGHEST) + br_ref[...]
    lane = lax.broadcasted_iota(jnp.int32, logits.shape, 1)
    gl = jnp.where(lane < N_GROUPS, logits, NEG)
    gmax = jnp.max(gl, axis=-1, keepdims=True)
    gsel = jnp.min(jnp.where(gl == gmax, lane, LANES), axis=-1, keepdims=True)
    gw = 1.0 / jnp.sum(jnp.exp(gl - gmax), axis=-1, keepdims=True)
    lo = N_GROUPS + EXPERTS_PER_GROUP * gsel
    el = jnp.where((lane >= lo) & (lane < lo + EXPERTS_PER_GROUP), logits, NEG)
    v1 = jnp.max(el, axis=-1, keepdims=True)
    i1 = jnp.min(jnp.where(el == v1, lane, LANES), axis=-1, keepdims=True)
    el2 = jnp.where(lane == i1, NEG, el)
    v2 = jnp.max(el2, axis=-1, keepdims=True)
    i2 = jnp.min(jnp.where(el2 == v2, lane, LANES), axis=-1, keepdims=True)
    e2 = jnp.exp(v2 - v1)
    w1 = gw / (1.0 + e2)
    w2 = gw * e2 / (1.0 + e2)
    rt = jnp.where(lane == 0, (i1 - N_GROUPS).astype(F32),
                   jnp.where(lane == 1, (i2 - N_GROUPS).astype(F32),
                             jnp.where(lane == 2, w1, jnp.where(lane == 3, w2, 0.0))))
    rt_ref[...] = rt


def merge_call(x, zt, ob, oc, gt1, sc2, sh2, n2, ws_bf, bs_tab, wbr_bf, wo_bf, wr, br, tm, seq_tiles):
    t, d = x.shape
    bw = ob.shape[1]
    mod_rows = gt1.shape[1]
    if mod_rows == 1:
        mod_map = lambda i: (i // seq_tiles, 0, 0)
    else:
        mod_map = lambda i: (0, 0, 0)
    gate0 = (8 * bw) // d
    full = lambda a: pl.BlockSpec(a.shape, lambda i: (0,) * a.ndim)
    return pl.pallas_call(
        _merge_kernel,
        grid=(t // tm,),
        in_specs=[
            pl.BlockSpec((tm, d), lambda i: (i, 0)),
            pl.BlockSpec((tm, bw), lambda i: (i, 0)),
            pl.BlockSpec((tm, bw), lambda i: (i, 1)),
            pl.BlockSpec((tm, bw), lambda i: (i, 0)),
            pl.BlockSpec((tm, bw), lambda i: (i, 0)),
            pl.BlockSpec((tm, d), lambda i: (i, gate0)),
            pl.BlockSpec((tm, d), lambda i: (i, gate0 + 1)),
            pl.BlockSpec((tm, d), lambda i: (i, gate0 + 2)),
            pl.BlockSpec((1, mod_rows, d), mod_map),
            pl.BlockSpec((1, mod_rows, d), mod_map),
            pl.BlockSpec((1, mod_rows, d), mod_map),
            full(n2), full(ws_bf), full(bs_tab), full(wbr_bf), full(wo_bf), full(wr), full(br),
        ],
        out_specs=[
            pl.BlockSpec((tm, d), lambda i: (i, 0)),
            pl.BlockSpec((tm, d), lambda i: (i, 0)),
            pl.BlockSpec((tm, LANES), lambda i: (i, 0)),
        ],
        out_shape=[jax.ShapeDtypeStruct((t, d), F32), jax.ShapeDtypeStruct((t, d), F32),
                   jax.ShapeDtypeStruct((t, LANES), F32)],
        compiler_params=_cparams(("arbitrary",)),
        name="merge",
    )(x, zt, zt, ob, oc, zt, zt, zt, gt1, sc2, sh2, n2, ws_bf, bs_tab, wbr_bf, wo_bf, wr, br)


def _gather_kernel(idx_ref, src_ref, o_ref, sem):
    rows = o_ref.shape[0]
    base = pl.program_id(0) * rows

    def row_copy(r, tok):
        return pltpu.make_async_copy(src_ref.at[pl.ds(tok, 1)], o_ref.at[pl.ds(r, 1)], sem)

    def issue(r, carry):
        row_copy(r, idx_ref[base + r]).start()
        return carry

    lax.fori_loop(0, rows, issue, 0, unroll=8)

    def drain(r, carry):
        row_copy(r, 0).wait()
        return carry

    lax.fori_loop(0, rows, drain, 0, unroll=8)


def gather_call(idx, src, rows_per_step):
    n = idx.shape[0]
    d = src.shape[1]
    grid_spec = pltpu.PrefetchScalarGridSpec(
        num_scalar_prefetch=1,
        grid=(n // rows_per_step,),
        in_specs=[pl.BlockSpec(memory_space=pl.ANY)],
        out_specs=pl.BlockSpec((rows_per_step, d), lambda i, idx: (i, 0)),
        scratch_shapes=[pltpu.SemaphoreType.DMA(())],
    )
    return pl.pallas_call(
        _gather_kernel,
        grid_spec=grid_spec,
        out_shape=jax.ShapeDtypeStruct((n, d), src.dtype),
        compiler_params=_cparams(("arbitrary",)),
        name="row_gather",
    )(idx, src)


def _expert_kernel(be_ref, nu_ref, x_ref, wg_ref, wu_ref, wd_ref, o_ref, wg_bf, wu_bf, wd_bf):
    i = pl.program_id(0)
    e = be_ref[i]
    prev = be_ref[jnp.maximum(i - 1, 0)]

    @pl.when((i == 0) | (e != prev))
    def _():
        wg_bf[...] = wg_ref[...].astype(BF16)
        wu_bf[...] = wu_ref[...].astype(BF16)
        wd_bf[...] = wd_ref[...].astype(BF16)

    @pl.when(i < nu_ref[0])
    def _():
        x = x_ref[...].astype(BF16)
        g = jnp.dot(x, wg_bf[...], preferred_element_type=F32)
        u = jnp.dot(x, wu_bf[...], preferred_element_type=F32)
        h = g * jax.nn.sigmoid(g) * u
        o_ref[...] = jnp.dot(h.astype(BF16), wd_bf[...], preferred_element_type=F32)

    @pl.when(i >= nu_ref[0])
    def _():
        o_ref[...] = jnp.zeros(o_ref.shape, o_ref.dtype)


def expert_call(blk_e, n_used, xs, wg, wu, wd, layer):
    n, d = xs.shape
    f = wg.shape[-1]
    nblk = n // MOE_ROWS
    grid_spec = pltpu.PrefetchScalarGridSpec(
        num_scalar_prefetch=2,
        grid=(nblk,),
        in_specs=[
            pl.BlockSpec((MOE_ROWS, d), lambda i, be, nu: (i, 0)),
            pl.BlockSpec((None, None, d, f), lambda i, be, nu: (layer, be[i], 0, 0)),
            pl.BlockSpec((None, None, d, f), lambda i, be, nu: (layer, be[i], 0, 0)),
            pl.BlockSpec((None, None, f, d), lambda i, be, nu: (layer, be[i], 0, 0)),
        ],
        out_specs=pl.BlockSpec((MOE_ROWS, d), lambda i, be, nu: (i, 0)),
        scratch_shapes=[pltpu.VMEM((d, f), BF16), pltpu.VMEM((d, f), BF16), pltpu.VMEM((f, d), BF16)],
    )
    return pl.pallas_call(
        _expert_kernel,
        grid_spec=grid_spec,
        out_shape=jax.ShapeDtypeStruct((n, d), F32),
        compiler_params=_cparams(("arbitrary",)),
        name="experts",
    )(blk_e, n_used, xs, wg, wu, wd)


def _combine_kernel(x1_ref, y0_ref, y1_ref, rt_ref, gt2_ref, fg_ref, o_ref, *, final):
    rt = rt_ref[...]
    x2 = x1_ref[...] + gt2_ref[0] * (rt[:, 2:3] * y0_ref[...] + rt[:, 3:4] * y1_ref[...])
    if final:
        x2 = x2 * lax.rsqrt(jnp.mean(x2 * x2, axis=-1, keepdims=True) + EPS) * fg_ref[...]
    o_ref[...] = x2


def combine_call(x1, ya, rt, gt2, fg, tm, seq_tiles, final):
    t, d = x1.shape
    mod_rows = gt2.shape[1]
    if mod_rows == 1:
        mod_map = lambda i: (i // seq_tiles, 0, 0)
    else:
        mod_map = lambda i: (0, 0, 0)
    nt = t // tm
    return pl.pallas_call(
        functools.partial(_combine_kernel, final=final),
        grid=(nt,),
        in_specs=[
            pl.BlockSpec((tm, d), lambda i: (i, 0)),
            pl.BlockSpec((tm, d), lambda i: (i, 0)),
            pl.BlockSpec((tm, d), lambda i: (i + nt, 0)),
            pl.BlockSpec((tm, LANES), lambda i: (i, 0)),
            pl.BlockSpec((1, mod_rows, d), mod_map),
            pl.BlockSpec((1, d), lambda i: (0, 0)),
        ],
        out_specs=pl.BlockSpec((tm, d), lambda i: (i, 0)),
        out_shape=jax.ShapeDtypeStruct((t, d), F32),
        compiler_params=_cparams(("arbitrary",)),
        name="combine",
    )(x1, ya, ya, rt, gt2, fg)


def moe_block(h2, x1, rt, gt2, fg, wg, wu, wd, layer, tm, seq_tiles, final):
    t, d = h2.shape
    a = t * EXPERT_TOPK
    eid = rt[:, :EXPERT_TOPK].astype(jnp.int32)
    flat = eid.T.reshape(-1)
    onehot = (flat[:, None] == jnp.arange(N_EXPERTS, dtype=jnp.int32)[None, :]).astype(jnp.int32)
    csum = jnp.cumsum(onehot, axis=0)
    rank = jnp.sum(onehot * csum, axis=1) - 1
    counts = csum[-1]
    padded = (counts + MOE_ROWS - 1) // MOE_ROWS * MOE_ROWS
    pad_ends = jnp.cumsum(padded)
    pad_starts = pad_ends - padded
    dest = (pad_starts[flat] + rank).astype(jnp.int32)
    nblk = -(-a // MOE_ROWS) + N_EXPERTS
    tok = jnp.tile(jnp.arange(t, dtype=jnp.int32), EXPERT_TOPK)
    src_tok = jnp.zeros((nblk * MOE_ROWS,), jnp.int32).at[dest].set(tok)
    n_used = (pad_ends[-1] // MOE_ROWS).astype(jnp.int32)
    blk_start = jnp.arange(nblk, dtype=jnp.int32) * MOE_ROWS
    blk_e = jnp.minimum(jnp.searchsorted(pad_ends, blk_start, side='right'), N_EXPERTS - 1).astype(jnp.int32)
    last_e = blk_e[jnp.maximum(n_used - 1, 0)]
    blk_e = jnp.where(jnp.arange(nblk) < n_used, blk_e, last_e)

    xs = gather_call(src_tok, h2, MOE_ROWS)
    yb = expert_call(blk_e, n_used.reshape(1), xs, wg, wu, wd, layer)
    rows = min(MOE_ROWS, a)
    ya = gather_call(dest, yb, rows)
    return combine_call(x1, ya, rt, gt2, fg, tm, seq_tiles, final)


def _split_heads(zt, col0, b, s, nu, dh):
    return zt[:, col0:col0 + nu * dh].reshape(b, s, nu, dh)


def layer_prompt(x, mods, lw, layer, rope, b, s, lam_init, final, fg):
    t, d = x.shape
    bw = d // 2
    sh1, sc1, gt1, sh2, sc2, gt2 = mods
    tm_in = min(1024, s)
    zt = inproj_call(x, sc1, sh1, lw['n1'], lw['win'], rope, lw['lng'], lw['lnb'], tm_in, s // tm_in)
    nb = s // ATT_BLOCK
    scale = HEAD_DIM ** -0.5
    mq = _split_heads(zt, 2 * bw, b, s, N_UNITS, HEAD_DIM)
    mk = _split_heads(zt, 3 * bw, b, s, N_UNITS, HEAD_DIM)
    mv = _split_heads(zt, 4 * bw, b, s, N_UNITS, HEAD_DIM)
    dq = _split_heads(zt, 5 * bw, b, s, N_UNITS, HEAD_DIM)
    dk = _split_heads(zt, 6 * bw, b, s, N_UNITS, HEAD_DIM)
    dv = _split_heads(zt, 7 * bw, b, s, N_UNITS // 2, 2 * HEAD_DIM)

    def q_t(q):
        return jnp.transpose(q * scale, (0, 2, 3, 1)).astype(BF16)

    def k_b(k):
        return jnp.transpose(k, (0, 2, 1, 3)).reshape(b, N_UNITS, nb, ATT_BLOCK, HEAD_DIM).astype(BF16)

    def v_t(v):
        nv, dvv = v.shape[2], v.shape[3]
        v5 = v.reshape(b, nb, ATT_BLOCK, nv, dvv)
        return jnp.transpose(v5, (0, 3, 1, 4, 2)).astype(BF16)

    ob_t = attn_call(q_t(mq), k_b(mk), v_t(mv), 1, True)
    oc_t = attn_call(q_t(dq), k_b(dk), v_t(dv), 2, False, lw['dlam'], lw['subg_col'], lam_init)
    ob = jnp.transpose(ob_t, (0, 3, 1, 2)).reshape(t, bw)
    oc = jnp.transpose(oc_t, (0, 3, 1, 2)).reshape(t, bw)

    tm = 256
    x1, h2, rt = merge_call(x, zt, ob, oc, gt1, sc2, sh2, lw['n2'], lw['ws'], lw['bs_tab'], lw['wbr'],
                            lw['wo'], lw['wr'], lw['br'], tm, s // tm)
    x2 = moe_block(h2, x1, rt, gt2, fg, lw['weg'], lw['weu'], lw['wed'], layer, tm, s // tm, final)
    outs = (zt[:, 3 * bw:4 * bw], zt[:, 4 * bw:5 * bw], zt[:, 6 * bw:7 * bw], zt[:, 7 * bw:8 * bw])
    return x2, outs


def layer_sample(x, mods, lw, layer, rope, db, ds, lam_init, final, fg, caches, page_table):
    t, d = x.shape
    bw = d // 2
    sh1, sc1, gt1, sh2, sc2, gt2 = mods
    zt = inproj_call(x, sc1, sh1, lw['n1'], lw['win'], rope, lw['lng'], lw['lnb'], t, 1)
    scale = HEAD_DIM ** -0.5
    eye = jnp.eye(N_UNITS, dtype=F32)

    def qbd(col0):
        q = zt[:, col0:col0 + bw].reshape(db, ds, N_UNITS, HEAD_DIM) * scale
        return jnp.einsum('btud,uw->btuwd', q, eye).reshape(db, ds * N_UNITS, bw)

    def pad_new(col0):
        a = zt[:, col0:col0 + bw].reshape(db, ds, bw)
        return jnp.pad(a, ((0, 0), (0, LANES - ds), (0, 0)))

    unit = np.arange(ds * N_UNITS) % N_UNITS
    col = np.arange(bw)
    om_moba = jnp.asarray((col[None, :] // HEAD_DIM == unit[:, None]).astype(np.float32))
    om_diff = jnp.asarray((col[None, :] // (2 * HEAD_DIM) == unit[:, None] // 2).astype(np.float32))
    ck, cv, cdk, cdv = caches
    ob = sattn_call(page_table, qbd(2 * bw), ck, cv, pad_new(3 * bw), pad_new(4 * bw), om_moba,
                    lw['dlam'], lw['subg_row'], layer, True, lam_init, ds).reshape(t, bw)
    oc = sattn_call(page_table, qbd(5 * bw), cdk, cdv, pad_new(6 * bw), pad_new(7 * bw), om_diff,
                    lw['dlam'], lw['subg_row'], layer, False, lam_init, ds).reshape(t, bw)
    x1, h2, rt = merge_call(x, zt, ob, oc, gt1, sc2, sh2, lw['n2'], lw['ws_s'], lw['bs_tab_s'], lw['wbr'],
                            lw['wo'], lw['wr'], lw['br'], t, 1)
    x2 = moe_block(h2, x1, rt, gt2, fg, lw['weg'], lw['weu'], lw['wed'], layer, t, 1, final)
    outs = (zt[:, 3 * bw:4 * bw], zt[:, 4 * bw:5 * bw], zt[:, 6 * bw:7 * bw], zt[:, 7 * bw:8 * bw],
            zt[:, bw:2 * bw])
    return x2, outs


def kernel(x_prompt, x_sample, cache_moba_k, cache_moba_v, cache_diff_k, cache_diff_v, page_table, c_prompt, c_sample, norm1_g, norm2_g, final_g, w_ada, b_ada, w_in, gm_ln_g, gm_ln_b, gm_ws, gm_bs, diff_lambda, diff_subln_g, w_branch, w_out, w_group, b_group, w_router, b_router, w_gate_e, w_up_e, w_down_e):
    b, s, d = x_prompt.shape
    db, ds, _ = x_sample.shape
    depth = w_in.shape[0]
    bw = d // 2
    gd = bw // GM_GROUPS
    past_len = page_table.shape[1] * cache_moba_k.shape[2]
    ts = db * ds
    assert s % ATT_BLOCK == 0 and ts % GM_CHUNK == 0 and GM_CHUNK % ds == 0

    rope_p = rope_tables(jnp.arange(s, dtype=jnp.int32))
    rope_s = rope_tables(jnp.tile(past_len + jnp.arange(ds, dtype=jnp.int32), db))
    npool, page = cache_moba_k.shape[1], cache_moba_k.shape[2]
    caches = tuple(c.reshape(depth, npool, page, bw)
                   for c in (cache_moba_k, cache_moba_v, cache_diff_k, cache_diff_v))

    nc = b + db
    ncp = -(-nc // 8) * 8
    c_all = jnp.pad(jnp.concatenate([c_prompt, c_sample], axis=0), ((0, ncp - nc), (0, 0)))
    tril = jnp.tril(jnp.ones((GM_CHUNK, GM_CHUNK), F32))
    fg = final_g.reshape(1, d)

    xp = x_prompt.reshape(b * s, d)
    xs = x_sample.reshape(ts, d)
    outs_p, outs_s = [], []
    for l in range(depth):
        lam_init = 0.8 - 0.6 * math.exp(-0.3 * l)
        final = l == depth - 1
        m = ada_call(c_all, w_ada, b_ada, l)
        parts = [m[:, i * d:(i + 1) * d] for i in range(6)]
        mods_p = tuple(p[:b].reshape(b, 1, d) for p in parts)
        mods_s = tuple(jnp.repeat(p[b:nc], ds, axis=0).reshape(1, ts, d) for p in parts)

        ws_tril = gm_ws[l] * tril[None]
        ws_small = ws_tril[:, :ds, :ds]
        ws_s = jnp.einsum('ab,gts->gatbs', jnp.eye(GM_CHUNK // ds, dtype=F32), ws_small)
        ws_s = ws_s.reshape(GM_GROUPS, GM_CHUNK, GM_CHUNK)
        bs_tab = jnp.repeat(gm_bs[l].T, gd, axis=1)
        bs_tab_s = jnp.tile(bs_tab[:ds], (GM_CHUNK // ds, 1))
        wr = jnp.pad(jnp.concatenate([w_group[l], w_router[l]], axis=1),
                     ((0, 0), (0, LANES - N_GROUPS - N_EXPERTS)))
        br = jnp.pad(jnp.concatenate([b_group[l], b_router[l]]), (0, LANES - N_GROUPS - N_EXPERTS))
        lw = dict(
            n1=norm1_g[l].reshape(1, d), n2=norm2_g[l].reshape(1, d),
            win=w_in[l].astype(BF16), lng=gm_ln_g[l].reshape(1, bw), lnb=gm_ln_b[l].reshape(1, bw),
            ws=ws_tril.astype(BF16), bs_tab=bs_tab, ws_s=ws_s.astype(BF16), bs_tab_s=bs_tab_s,
            dlam=diff_lambda[l], subg_col=diff_subln_g[l].reshape(-1, 1), subg_row=diff_subln_g[l].reshape(1, -1),
            wbr=w_branch[l].astype(BF16), wo=w_out[l].astype(BF16), wr=wr, br=br.reshape(1, LANES),
            weg=w_gate_e, weu=w_up_e, wed=w_down_e,
        )
        xp, op = layer_prompt(xp, mods_p, lw, l, rope_p, b, s, lam_init, final, fg)
        xs, os_ = layer_sample(xs, mods_s, lw, l, rope_s, db, ds, lam_init, final, fg, caches, page_table)
        outs_p.append(op)
        outs_s.append(os_)

    nh = N_UNITS
    stack = lambda lst, i, shape: jnp.stack([o[i].reshape(shape) for o in lst])
    return (
        xp.reshape(b, s, d), xs.reshape(db, ds, d),
        stack(outs_p, 0, (b, s, nh, HEAD_DIM)), stack(outs_p, 1, (b, s, nh, HEAD_DIM)),
        stack(outs_p, 2, (b, s, nh // 2, 2, HEAD_DIM)), stack(outs_p, 3, (b, s, nh // 2, 2 * HEAD_DIM)),
        stack(outs_s, 0, (db, ds, nh, HEAD_DIM)), stack(outs_s, 1, (db, ds, nh, HEAD_DIM)),
        stack(outs_s, 2, (db, ds, nh // 2, 2, HEAD_DIM)), stack(outs_s, 3, (db, ds, nh // 2, 2 * HEAD_DIM)),
        stack(outs_s, 4, (db, ds, bw)),
    )
```

```python
import functools
import math

import jax
import jax.numpy as jnp
import numpy as np
from jax import lax
from jax.experimental import pallas as pl
from jax.experimental.pallas import tpu as pltpu

F32 = jnp.float32
BF16 = jnp.bfloat16
HIGHEST = lax.Precision.HIGHEST

EPS = 1e-6
NEG = -1e30
BIG = 1e30

N_BRANCHES = 3
GM_GROUPS = 4
GM_CHUNK = 128
N_UNITS = 8
HEAD_DIM = 64
ROT = HEAD_DIM // 4
ROPE_THETA = 500000.0
ATT_BLOCK = 256
ATT_CHAINS = 4
MOBA_TOPK = 3
SAMPLE_BLOCKS_PER_STEP = 4
N_GROUPS = 4
EXPERTS_PER_GROUP = 8
N_EXPERTS = N_GROUPS * EXPERTS_PER_GROUP
EXPERT_TOPK = 2
MOE_ROWS = 256
LANES = 128
VMEM_LIMIT = 48 * 1024 * 1024
NT_DIMS = (((1,), (1,)), ((), ()))


def _cparams(sem):
    return pltpu.CompilerParams(dimension_semantics=sem, vmem_limit_bytes=VMEM_LIMIT)


def _ada_kernel(c_ref, w_ref, b_ref, o_ref):
    c = c_ref[...]
    a = c * jax.nn.sigmoid(c)
    o_ref[...] = jnp.dot(a, w_ref[...], preferred_element_type=F32, precision=HIGHEST) + b_ref[...]


def ada_call(c_all, w_ada, b_ada, layer):
    m, d = c_all.shape
    n = w_ada.shape[-1]
    tn = n // 4
    return pl.pallas_call(
        _ada_kernel,
        grid=(n // tn,),
        in_specs=[
            pl.BlockSpec((m, d), lambda j: (0, 0)),
            pl.BlockSpec((None, d, tn), lambda j: (layer, 0, j)),
            pl.BlockSpec((None, 1, tn), lambda j: (layer, 0, j)),
        ],
        out_specs=pl.BlockSpec((m, tn), lambda j: (0, j)),
        out_shape=jax.ShapeDtypeStruct((m, n), F32),
        compiler_params=_cparams(("arbitrary",)),
        name="ada",
    )(c_all, w_ada, b_ada.reshape(b_ada.shape[0], 1, n))


def _norm_mod(x_ref, sc_ref, sh_ref, g_ref):
    x = x_ref[...]
    y = x * lax.rsqrt(jnp.mean(x * x, axis=-1, keepdims=True) + EPS) * g_ref[...]
    return (y * (1.0 + sc_ref[0]) + sh_ref[0]).astype(BF16)


def _gelu_ln(z, lng_ref, lnb_ref):
    a = jax.nn.gelu(z)
    mu = jnp.mean(a, axis=-1, keepdims=True)
    ac = a - mu
    var = jnp.mean(ac * ac, axis=-1, keepdims=True)
    return ac * lax.rsqrt(var + EPS) * lng_ref[...] + lnb_ref[...]


def _rope(z, rc_ref, rs1_ref, rs2_ref):
    width = z.shape[-1]
    reps = width // LANES
    c = jnp.concatenate([rc_ref[...]] * reps, axis=1)
    s1 = jnp.concatenate([rs1_ref[...]] * reps, axis=1)
    s2 = jnp.concatenate([rs2_ref[...]] * reps, axis=1)
    up = pltpu.roll(z, width - ROT // 2, 1)
    dn = pltpu.roll(z, ROT // 2, 1)
    return z * c + up * s1 + dn * s2


def _inproj_kernel(x_ref, sc_ref, sh_ref, g_ref, w_ref, rc_ref, rs1_ref, rs2_ref, lng_ref, lnb_ref,
                   o_ref, h_scr):
    j = pl.program_id(1)

    @pl.when(j == 0)
    def _():
        h_scr[...] = _norm_mod(x_ref, sc_ref, sh_ref, g_ref)

    z = jnp.dot(h_scr[...], w_ref[...], preferred_element_type=F32)

    @pl.when(j == 0)
    def _():
        o_ref[...] = jax.nn.gelu(z)

    @pl.when(j == 1)
    def _():
        o_ref[...] = _gelu_ln(z, lng_ref, lnb_ref)

    @pl.when((j == 2) | (j == 3) | (j == 5) | (j == 6))
    def _():
        o_ref[...] = _rope(z, rc_ref, rs1_ref, rs2_ref)

    @pl.when((j == 4) | (j == 7))
    def _():
        o_ref[...] = z

    @pl.when(j >= 8)
    def _():
        o_ref[...] = jax.nn.sigmoid(z)


def _inproj_attn_kernel(x_ref, sc_ref, sh_ref, g_ref, w_ref, rc_ref, rs1_ref, rs2_ref, lng_ref, lnb_ref,
                        uv_ref, gates_ref, mk_ref, mv_ref, dk_ref, dv_ref,
                        mqt_ref, dqt_ref, mkb_ref, dkb_ref, mvt_ref, dvt_ref, h_scr):
    j = pl.program_id(1)
    scale = HEAD_DIM ** -0.5 * math.log2(math.e)

    @pl.when(j == 0)
    def _():
        h_scr[...] = _norm_mod(x_ref, sc_ref, sh_ref, g_ref)

    z = jnp.dot(h_scr[...], w_ref[...], preferred_element_type=F32)

    def put_qt(ref, zr):
        zt = (zr * scale).T.astype(BF16)
        for u in range(ref.shape[0]):
            ref[u] = zt[u * HEAD_DIM:(u + 1) * HEAD_DIM, :]

    def put_kb(ref, zr):
        zb = zr.astype(BF16)
        for u in range(ref.shape[0]):
            for kb in range(ref.shape[1]):
                ref[u, kb] = zb[kb * ATT_BLOCK:(kb + 1) * ATT_BLOCK, u * HEAD_DIM:(u + 1) * HEAD_DIM]

    def put_vt(ref, zv):
        zt = zv.T.astype(BF16)
        dv = ref.shape[2]
        for n in range(ref.shape[0]):
            for kb in range(ref.shape[1]):
                ref[n, kb] = zt[n * dv:(n + 1) * dv, kb * ATT_BLOCK:(kb + 1) * ATT_BLOCK]

    @pl.when(j == 0)
    def _():
        uv_ref[...] = jax.nn.gelu(z)

    @pl.when(j == 1)
    def _():
        uv_ref[...] = _gelu_ln(z, lng_ref, lnb_ref)

    @pl.when(j == 2)
    def _():
        put_qt(mqt_ref, _rope(z, rc_ref, rs1_ref, rs2_ref))

    @pl.when(j == 3)
    def _():
        zr = _rope(z, rc_ref, rs1_ref, rs2_ref)
        mk_ref[...] = zr
        put_kb(mkb_ref, zr)

    @pl.when(j == 4)
    def _():
        mv_ref[...] = z
        put_vt(mvt_ref, z)

    @pl.when(j == 5)
    def _():
        put_qt(dqt_ref, _rope(z, rc_ref, rs1_ref, rs2_ref))

    @pl.when(j == 6)
    def _():
        zr = _rope(z, rc_ref, rs1_ref, rs2_ref)
        dk_ref[...] = zr
        put_kb(dkb_ref, zr)

    @pl.when(j == 7)
    def _():
        dv_ref[...] = z
        put_vt(dvt_ref, z)

    @pl.when(j >= 8)
    def _():
        gates_ref[...] = jax.nn.sigmoid(z)


def _inproj_in_specs(tm, d, tn, mod_rows, seq_tiles):
    if mod_rows == 1:
        mod_map = lambda i, j: (i // seq_tiles, 0, 0)
    else:
        mod_map = lambda i, j: (0, 0, 0)
    rope_map = lambda i, j: (i % seq_tiles, 0)
    return [
        pl.BlockSpec((tm, d), lambda i, j: (i, 0)),
        pl.BlockSpec((1, mod_rows, d), mod_map),
        pl.BlockSpec((1, mod_rows, d), mod_map),
        pl.BlockSpec((1, d), lambda i, j: (0, 0)),
        pl.BlockSpec((d, tn), lambda i, j: (0, j)),
        pl.BlockSpec((tm, LANES), rope_map),
        pl.BlockSpec((tm, LANES), rope_map),
        pl.BlockSpec((tm, LANES), rope_map),
        pl.BlockSpec((1, tn), lambda i, j: (0, 0)),
        pl.BlockSpec((1, tn), lambda i, j: (0, 0)),
    ]


def inproj_call(x, sc, sh, g, w_bf, rope, lng, lnb, tm, seq_tiles):
    t, d = x.shape
    n = w_bf.shape[1]
    tn = 512
    rc, rs1, rs2 = rope
    return pl.pallas_call(
        _inproj_kernel,
        grid=(t // tm, n // tn),
        in_specs=_inproj_in_specs(tm, d, tn, sc.shape[1], seq_tiles),
        out_specs=pl.BlockSpec((tm, tn), lambda i, j: (i, j)),
        out_shape=jax.ShapeDtypeStruct((t, n), F32),
        scratch_shapes=[pltpu.VMEM((tm, d), BF16)],
        compiler_params=_cparams(("arbitrary", "arbitrary")),
        name="inproj",
    )(x, sc, sh, g, w_bf, rc, rs1, rs2, lng, lnb)


def inproj_attn_call(x, sc, sh, g, w_bf, rope, lng, lnb, tm, b, s):
    t, d = x.shape
    tn = 512
    n = w_bf.shape[1]
    seq_tiles = s // tm
    nb = s // ATT_BLOCK
    kbt = tm // ATT_BLOCK
    nu, dh = N_UNITS, HEAD_DIM
    rc, rs1, rs2 = rope
    bi = lambda i: i // seq_tiles
    si = lambda i: i % seq_tiles
    row_spec = lambda w: pl.BlockSpec((tm, w), lambda i, j: (i, 0))
    qt_spec = pl.BlockSpec((None, nu, dh, tm), lambda i, j: (bi(i), 0, 0, si(i)))
    kb_spec = pl.BlockSpec((None, nu, kbt, ATT_BLOCK, dh), lambda i, j: (bi(i), 0, si(i), 0, 0))

    def vt_spec(nv):
        return pl.BlockSpec((None, nv, kbt, tn // nv, ATT_BLOCK), lambda i, j: (bi(i), 0, si(i), 0, 0))

    sds = jax.ShapeDtypeStruct
    return pl.pallas_call(
        _inproj_attn_kernel,
        grid=(t // tm, n // tn),
        in_specs=_inproj_in_specs(tm, d, tn, sc.shape[1], seq_tiles),
        out_specs=[
            pl.BlockSpec((tm, tn), lambda i, j: (i, jnp.minimum(j, 1))),
            pl.BlockSpec((tm, tn), lambda i, j: (i, jnp.maximum(j - 8, 0))),
            row_spec(tn), row_spec(tn), row_spec(tn), row_spec(tn),
            qt_spec, qt_spec, kb_spec, kb_spec, vt_spec(nu), vt_spec(nu // 2),
        ],
        out_shape=[
            sds((t, 2 * tn), F32), sds((t, n - 8 * tn), F32),
            sds((t, tn), F32), sds((t, tn), F32), sds((t, tn), F32), sds((t, tn), F32),
            sds((b, nu, dh, s), BF16), sds((b, nu, dh, s), BF16),
            sds((b, nu, nb, ATT_BLOCK, dh), BF16), sds((b, nu, nb, ATT_BLOCK, dh), BF16),
            sds((b, nu, nb, dh, ATT_BLOCK), BF16), sds((b, nu // 2, nb, 2 * dh, ATT_BLOCK), BF16),
        ],
        scratch_shapes=[pltpu.VMEM((tm, d), BF16)],
        compiler_params=_cparams(("arbitrary", "arbitrary")),
        name="inproj_attn",
    )(x, sc, sh, g, w_bf, rc, rs1, rs2, lng, lnb)


def rope_tables(pos):
    half = ROT // 2
    inv = ROPE_THETA ** (-(jnp.arange(half, dtype=F32) * 2.0 / ROT))
    ang = pos.astype(F32)[:, None] * inv[None, :]
    cos, sin = jnp.cos(ang), jnp.sin(ang)
    ones = jnp.ones((pos.shape[0], HEAD_DIM - ROT), F32)
    zeros8 = jnp.zeros((pos.shape[0], half), F32)
    zrest = jnp.zeros_like(ones)
    c = jnp.concatenate([cos, cos, ones], axis=1)
    s1 = jnp.concatenate([-sin, zeros8, zrest], axis=1)
    s2 = jnp.concatenate([zeros8, sin, zrest], axis=1)
    tile = lambda a: jnp.concatenate([a] * (LANES // HEAD_DIM), axis=1)
    return tile(c), tile(s1), tile(s2)


def _diff_lambda(dl_ref, lam_init):
    dl = dl_ref[...]
    return (jnp.exp(jnp.sum(dl[0:1] * dl[1:2], axis=1, keepdims=True))
            - jnp.exp(jnp.sum(dl[2:3] * dl[3:4], axis=1, keepdims=True)) + lam_init)


def _attn_kernel(*refs, nmaps, use_sel, lam_init):
    if use_sel:
        q_ref, k_ref, v_ref, o_ref, kmean_scr, sel_scr = refs
    else:
        q_ref, k_ref, v_ref, dl_ref, sg_ref, o_ref = refs
    qi = pl.program_id(2)
    nchain = q_ref.shape[0]
    nb = k_ref.shape[1]
    tq = q_ref.shape[-1]
    blk = k_ref.shape[2]

    if use_sel:
        @pl.when(qi == 0)
        def _():
            for c in range(nchain):
                kmean_scr[c] = jnp.mean(k_ref[c].astype(F32), axis=1)

        row = lax.broadcasted_iota(jnp.int32, (nb, tq), 0)
        for c in range(nchain):
            gate = jnp.dot(kmean_scr[c], q_ref[c].astype(F32), preferred_element_type=F32,
                           precision=HIGHEST)
            past = row < qi
            gate = jnp.where(past, gate, NEG)
            sel = jnp.zeros((nb, tq), F32)
            for _ in range(MOBA_TOPK):
                top = jnp.max(gate, axis=0, keepdims=True)
                first = jnp.min(jnp.where(gate == top, row, nb), axis=0, keepdims=True)
                hit = row == first
                sel = jnp.where(hit & past, 1.0, sel)
                gate = jnp.where(hit, -jnp.inf, gate)
            sel_scr[c] = sel

    kr = lax.broadcasted_iota(jnp.int32, (blk, tq), 0)
    qc = lax.broadcasted_iota(jnp.int32, (blk, tq), 1)
    qts = [q_ref[c] for c in range(nchain)]
    ss = [jnp.dot(k_ref[c, qi], qts[c], preferred_element_type=F32) for c in range(nchain)]
    stats = []
    for c in range(nchain):
        s = jnp.where(kr <= qc, ss[c], NEG)
        m0 = jnp.max(s, axis=0, keepdims=True)
        p = jnp.exp2(s - m0)
        stats.append((m0, jnp.sum(p, axis=0, keepdims=True), p.astype(BF16)))
    carry0 = tuple(
        (stats[c][0], stats[c][1],
         jnp.dot(v_ref[c // nmaps, qi], stats[c][2], preferred_element_type=F32))
        for c in range(nchain))

    def kv_body(kj, carry):
        ss = [jnp.dot(k_ref[c, kj], qts[c], preferred_element_type=F32) for c in range(nchain)]
        upd = []
        for c in range(nchain):
            m, l, _ = carry[c]
            smax = jnp.max(ss[c], axis=0, keepdims=True)
            if use_sel:
                on = sel_scr[c, pl.ds(kj, 1), :] > 0.5
                m_new = jnp.where(on, jnp.maximum(m, smax), m)
                m_use = jnp.where(on, m_new, BIG)
            else:
                m_new = jnp.maximum(m, smax)
                m_use = m_new
            p = jnp.exp2(ss[c] - m_use)
            alpha = jnp.exp2(m - m_new)
            upd.append((m_new, alpha * l + jnp.sum(p, axis=0, keepdims=True), alpha, p.astype(BF16)))
        return tuple(
            (upd[c][0], upd[c][1],
             upd[c][2] * carry[c][2] + jnp.dot(v_ref[c // nmaps, kj], upd[c][3], preferred_element_type=F32))
            for c in range(nchain))

    carry = lax.fori_loop(0, qi, kv_body, tuple(carry0))
    outs = [acc / l for (_, l, acc) in carry]

    if use_sel:
        for c in range(nchain):
            o_ref[c] = outs[c]
    else:
        lam = _diff_lambda(dl_ref, lam_init)
        for hh in range(nchain // 2):
            o = outs[2 * hh] - lam * outs[2 * hh + 1]
            ms = jnp.mean(o * o, axis=0, keepdims=True)
            o_ref[hh] = o * lax.rsqrt(ms + EPS) * sg_ref[...] * (1.0 - lam_init)


def attn_call(qt, k, vt, nmaps, use_sel, dlam=None, subg=None, lam_init=0.0):
    b, nu, dh, s = qt.shape
    nb, blk = k.shape[2], k.shape[3]
    nv, dv = vt.shape[1], vt.shape[3]
    tq = blk
    nc = ATT_CHAINS
    nvb = nc // nmaps
    kern = functools.partial(_attn_kernel, nmaps=nmaps, use_sel=use_sel, lam_init=lam_init)
    in_specs = [
        pl.BlockSpec((None, nc, dh, tq), lambda bi, n, qi: (bi, n, 0, qi)),
        pl.BlockSpec((None, nc, nb, blk, dh), lambda bi, n, qi: (bi, n, 0, 0, 0)),
        pl.BlockSpec((None, nvb, nb, dv, blk), lambda bi, n, qi: (bi, n, 0, 0, 0)),
    ]
    args = [qt, k, vt]
    scratch = []
    if use_sel:
        scratch = [pltpu.VMEM((nc, nb, dh), F32), pltpu.VMEM((nc, nb, tq), F32)]
    else:
        in_specs += [pl.BlockSpec(dlam.shape, lambda bi, n, qi: (0, 0)),
                     pl.BlockSpec(subg.shape, lambda bi, n, qi: (0, 0))]
        args += [dlam, subg]
    return pl.pallas_call(
        kern,
        grid=(b, nu // nc, s // tq),
        in_specs=in_specs,
        out_specs=pl.BlockSpec((None, nvb, dv, tq), lambda bi, n, qi: (bi, n, 0, qi)),
        out_shape=jax.ShapeDtypeStruct((b, nv, dv, s), F32),
        scratch_shapes=scratch,
        compiler_params=_cparams(("arbitrary", "arbitrary", "arbitrary")),
        name="moba_attn" if use_sel else "diff_attn",
    )(*args)


def _sattn_kernel(pt_ref, *refs, use_sel, nblk, bps, n_new, lam_init):
    del pt_ref
    npg = 2 * bps
    qbd_ref = refs[0]
    kt_refs = refs[1:1 + npg]
    v_refs = refs[1 + npg:1 + 2 * npg]
    kn_ref, vn_ref, om_ref, dl_ref, sg_ref, o_ref, opart, g_scr, m_scr, l_scr = refs[1 + 2 * npg:]
    j = pl.program_id(1)
    qf = qbd_ref[0]
    rows, wdt = qf.shape
    q_hi = qf.astype(BF16)
    q_lo = (qf - q_hi.astype(F32)).astype(BF16)
    lhs = jnp.concatenate([q_hi, q_lo], axis=0)
    lane = lax.broadcasted_iota(jnp.int32, (rows, LANES), 1)
    om = om_ref[...]

    @pl.when(j == 0)
    def _():
        g_scr[...] = jnp.full((rows, LANES), NEG, F32)
        m_scr[...] = jnp.full((rows, LANES), NEG, F32)
        l_scr[...] = jnp.zeros((rows, LANES), F32)

    def scores(kt):
        s2 = jnp.dot(lhs, kt, preferred_element_type=F32)
        return s2[:rows] + s2[rows:]

    gacc, macc, lacc = g_scr[...], m_scr[...], l_scr[...]
    for bb in range(bps):
        blk_id = j * bps + bb
        s = jnp.concatenate([scores(kt_refs[2 * bb + w][...].astype(BF16)) for w in range(2)], axis=1)
        mj = jnp.max(s, axis=-1, keepdims=True)
        p = jnp.exp(s - mj)
        lj = jnp.sum(p, axis=-1, keepdims=True)
        pb = p.astype(BF16)
        oj = None
        for w in range(2):
            pw = pb[:, w * LANES:(w + 1) * LANES]
            vref = v_refs[2 * bb + w]
            if use_sel:
                ow = lax.dot_general(pw, vref[...].astype(BF16), NT_DIMS, preferred_element_type=F32)
            else:
                nh = vref.shape[0] // LANES
                ow = jnp.concatenate(
                    [jnp.dot(pw, vref[pl.ds(hh, LANES, stride=nh), :].astype(BF16),
                             preferred_element_type=F32) for hh in range(nh)], axis=1)
            oj = ow if oj is None else oj + ow
        opart[blk_id] = oj * om
        hit = lane == blk_id
        if use_sel:
            gacc = jnp.where(hit, jnp.sum(s, axis=-1, keepdims=True), gacc)
        macc = jnp.where(hit, mj, macc)
        lacc = jnp.where(hit, lj, lacc)
    g_scr[...] = gacc
    m_scr[...] = macc
    l_scr[...] = lacc

    @pl.when(j == nblk // bps - 1)
    def _():
        kn = kn_ref[0].astype(BF16)
        vn = vn_ref[0].astype(BF16)
        so2 = lax.dot_general(lhs, kn, NT_DIMS, preferred_element_type=F32)
        so = so2[:rows] + so2[rows:]
        rtok = lax.broadcasted_iota(jnp.int32, (rows, LANES), 0) // N_UNITS
        so = jnp.where((lane <= rtok) & (lane < n_new), so, NEG)
        mo = jnp.max(so, axis=-1, keepdims=True)
        po = jnp.exp(so - mo)
        lo = jnp.sum(po, axis=-1, keepdims=True)
        oo = jnp.dot(po.astype(BF16), vn, preferred_element_type=F32) * om

        inb = lane < nblk
        if use_sel:
            g = jnp.where(inb, gacc, -jnp.inf)
            selm = jnp.zeros((rows, LANES), F32)
            for _ in range(min(MOBA_TOPK, nblk)):
                top = jnp.max(g, axis=-1, keepdims=True)
                first = jnp.min(jnp.where(g == top, lane, LANES), axis=-1, keepdims=True)
                hit = lane == first
                selm = jnp.where(hit, 1.0, selm)
                g = jnp.where(hit, -jnp.inf, g)
            selb = selm > 0.5
        else:
            selb = inb
        mt = jnp.maximum(jnp.max(jnp.where(selb, macc, NEG), axis=-1, keepdims=True), mo)
        w = jnp.where(selb, jnp.exp(macc - mt), 0.0)
        wo = jnp.exp(mo - mt)
        ltot = jnp.sum(w * lacc, axis=-1, keepdims=True) + wo * lo
        acc = wo * oo
        for jb in range(nblk):
            acc = acc + w[:, jb:jb + 1] * opart[jb]
        res = acc / ltot
        if not use_sel:
            lam = _diff_lambda(dl_ref, lam_init)
            rr = lax.broadcasted_iota(jnp.int32, (rows, 1), 0)
            res = res * jnp.where(rr % 2 == 0, 1.0, -lam)
        out = jnp.sum(res.reshape(rows // N_UNITS, N_UNITS, wdt), axis=1)
        if not use_sel:
            dv = sg_ref.shape[-1]
            segs = []
            for hh in range(wdt // dv):
                seg = out[:, hh * dv:(hh + 1) * dv]
                ms = jnp.mean(seg * seg, axis=-1, keepdims=True)
                segs.append(seg * lax.rsqrt(ms + EPS) * sg_ref[...] * (1.0 - lam_init))
            out = jnp.concatenate(segs, axis=1)
        o_ref[0] = out


def sattn_call(page_table, qbd, cache_kt, cache_v, knew, vnew, omask, dlam, subg, layer, use_sel,
               lam_init, n_new):
    db, rows, w = qbd.shape
    page = cache_kt.shape[3]
    n_pages = page_table.shape[1]
    nblk = n_pages * page // ATT_BLOCK
    bps = SAMPLE_BLOCKS_PER_STEP
    assert ATT_BLOCK == 2 * page and page == LANES and nblk <= LANES and nblk % bps == 0
    pt_flat = page_table.reshape(-1)

    def page_spec(arr, which):
        return pl.BlockSpec((None, None) + arr.shape[2:],
                            lambda b, j, pt: (layer, pt[b * n_pages + 2 * bps * j + which], 0, 0))

    kern = functools.partial(_sattn_kernel, use_sel=use_sel, nblk=nblk, bps=bps, n_new=n_new,
                             lam_init=lam_init)
    n_tok = rows // N_UNITS
    grid_spec = pltpu.PrefetchScalarGridSpec(
        num_scalar_prefetch=1,
        grid=(db, nblk // bps),
        in_specs=(
            [pl.BlockSpec((1, rows, w), lambda b, j, pt: (b, 0, 0))]
            + [page_spec(cache_kt, i) for i in range(2 * bps)]
            + [page_spec(cache_v, i) for i in range(2 * bps)]
            + [pl.BlockSpec((1, LANES, w), lambda b, j, pt: (b, 0, 0)),
               pl.BlockSpec((1, LANES, w), lambda b, j, pt: (b, 0, 0)),
               pl.BlockSpec((rows, w), lambda b, j, pt: (0, 0)),
               pl.BlockSpec(dlam.shape, lambda b, j, pt: (0, 0)),
               pl.BlockSpec(subg.shape, lambda b, j, pt: (0, 0))]
        ),
        out_specs=pl.BlockSpec((1, n_tok, w), lambda b, j, pt: (b, 0, 0)),
        scratch_shapes=[pltpu.VMEM((nblk, rows, w), F32), pltpu.VMEM((rows, LANES), F32),
                        pltpu.VMEM((rows, LANES), F32), pltpu.VMEM((rows, LANES), F32)],
    )
    return pl.pallas_call(
        kern,
        grid_spec=grid_spec,
        out_shape=jax.ShapeDtypeStruct((db, n_tok, w), F32),
        compiler_params=_cparams(("arbitrary", "arbitrary")),
        name="moba_sample" if use_sel else "diff_sample",
    )(pt_flat, qbd, *([cache_kt] * (2 * bps)), *([cache_v] * (2 * bps)), knew, vnew, omask, dlam, subg)


def _merge_kernel(x_ref, u_ref, v_ref, ob_ref, oc_ref, ga_ref, gb_ref, gc_ref, gt1_ref, sc2_ref,
                  sh2_ref, n2_ref, ws_ref, bs_ref, wbr_ref, wo_ref, wr_ref, br_ref,
                  x1_ref, h2_ref, rt_ref, *, o_transposed):
    tm = x_ref.shape[0]
    gd = ws_ref.shape[-1]
    v = v_ref[...].astype(BF16)
    chunks = []
    for c in range(tm // GM_CHUNK):
        r0 = c * GM_CHUNK
        cols = [jnp.dot(ws_ref[g], v[r0:r0 + GM_CHUNK, g * gd:(g + 1) * gd], preferred_element_type=F32)
                for g in range(GM_GROUPS)]
        chunks.append(jnp.concatenate(cols, axis=1) + bs_ref[...])
    mixed = jnp.concatenate(chunks, axis=0) if len(chunks) > 1 else chunks[0]
    o_a = u_ref[...] * mixed
    if o_transposed:
        o_b = ob_ref[...].T
        o_c = oc_ref[...].T
    else:
        o_b = ob_ref[...]
        o_c = oc_ref[...]
    merged = (ga_ref[...] * jnp.dot(o_a.astype(BF16), wbr_ref[0], preferred_element_type=F32)
              + gb_ref[...] * jnp.dot(o_b.astype(BF16), wbr_ref[1], preferred_element_type=F32)
              + gc_ref[...] * jnp.dot(o_c.astype(BF16), wbr_ref[2], preferred_element_type=F32))
    y = jnp.dot(merged.astype(BF16), wo_ref[...], preferred_element_type=F32)
    x1 = x_ref[...] + gt1_ref[0] * y
    x1_ref[...] = x1
    h2 = x1 * lax.rsqrt(jnp.mean(x1 * x1, axis=-1, keepdims=True) + EPS) * n2_ref[...]
    h2 = h2 * (1.0 + sc2_ref[0]) + sh2_ref[0]
    h2_ref[...] = h2

    logits = jnp.dot(h2, wr_ref[...], preferred_element_type=F32, precision=HIGHEST) + br_ref[...]
    lane = lax.broadcasted_iota(jnp.int32, logits.shape, 1)
    gl = jnp.where(lane < N_GROUPS, logits, NEG)
    gmax = jnp.max(gl, axis=-1, keepdims=True)
    gsel = jnp.min(jnp.where(gl == gmax, lane, LANES), axis=-1, keepdims=True)
    gw = 1.0 / jnp.sum(jnp.exp(gl - gmax), axis=-1, keepdims=True)
    lo = N_GROUPS + EXPERTS_PER_GROUP * gsel
    el = jnp.where((lane >= lo) & (lane < lo + EXPERTS_PER_GROUP), logits, NEG)
    v1 = jnp.max(el, axis=-1, keepdims=True)
    i1 = jnp.min(jnp.where(el == v1, lane, LANES), axis=-1, keepdims=True)
    el2 = jnp.where(lane == i1, NEG, el)
    v2 = jnp.max(el2, axis=-1, keepdims=True)
    i2 = jnp.min(jnp.where(el2 == v2, lane, LANES), axis=-1, keepdims=True)
    e2 = jnp.exp(v2 - v1)
    w1 = gw / (1.0 + e2)
    w2 = gw * e2 / (1.0 + e2)
    rt = jnp.where(lane == 0, (i1 - N_GROUPS).astype(F32),
                   jnp.where(lane == 1, (i2 - N_GROUPS).astype(F32),
                             jnp.where(lane == 2, w1, jnp.where(lane == 3, w2, 0.0))))
    rt_ref[...] = rt


def merge_call(x, uv, gates, ob, oc, gt1, sc2, sh2, n2, ws_bf, bs_tab, wbr_bf, wo_bf, wr, br, tm,
               seq_tiles, uv_col0, gate_col0, o_transposed):
    t, d = x.shape
    bw = d // 2
    mod_rows = gt1.shape[1]
    if mod_rows == 1:
        mod_map = lambda i: (i // seq_tiles, 0, 0)
    else:
        mod_map = lambda i: (0, 0, 0)
    if o_transposed:
        o_spec = pl.BlockSpec((None, bw, tm), lambda i: (i // seq_tiles, 0, i % seq_tiles))
    else:
        o_spec = pl.BlockSpec((tm, bw), lambda i: (i, 0))
    full = lambda a: pl.BlockSpec(a.shape, lambda i: (0,) * a.ndim)
    return pl.pallas_call(
        functools.partial(_merge_kernel, o_transposed=o_transposed),
        grid=(t // tm,),
        in_specs=[
            pl.BlockSpec((tm, d), lambda i: (i, 0)),
            pl.BlockSpec((tm, bw), lambda i: (i, uv_col0)),
            pl.BlockSpec((tm, bw), lambda i: (i, uv_col0 + 1)),
            o_spec, o_spec,
            pl.BlockSpec((tm, d), lambda i: (i, gate_col0)),
            pl.BlockSpec((tm, d), lambda i: (i, gate_col0 + 1)),
            pl.BlockSpec((tm, d), lambda i: (i, gate_col0 + 2)),
            pl.BlockSpec((1, mod_rows, d), mod_map),
            pl.BlockSpec((1, mod_rows, d), mod_map),
            pl.BlockSpec((1, mod_rows, d), mod_map),
            full(n2), full(ws_bf), full(bs_tab), full(wbr_bf), full(wo_bf), full(wr), full(br),
        ],
        out_specs=[
            pl.BlockSpec((tm, d), lambda i: (i, 0)),
            pl.BlockSpec((tm, d), lambda i: (i, 0)),
            pl.BlockSpec((tm, LANES), lambda i: (i, 0)),
        ],
        out_shape=[jax.ShapeDtypeStruct((t, d), F32), jax.ShapeDtypeStruct((t, d), F32),
                   jax.ShapeDtypeStruct((t, LANES), F32)],
        compiler_params=_cparams(("arbitrary",)),
        name="merge",
    )(x, uv, uv, ob, oc, gates, gates, gates, gt1, sc2, sh2, n2, ws_bf, bs_tab, wbr_bf, wo_bf, wr, br)


def _gather_kernel(idx_ref, src_ref, o_ref, sem):
    rows = o_ref.shape[0]
    base = pl.program_id(0) * rows

    def row_copy(r, tok):
        return pltpu.make_async_copy(src_ref.at[pl.ds(tok, 1)], o_ref.at[pl.ds(r, 1)], sem)

    def issue(r, carry):
        row_copy(r, idx_ref[base + r]).start()
        return carry

    lax.fori_loop(0, rows, issue, 0, unroll=8)

    def drain(r, carry):
        row_copy(r, 0).wait()
        return carry

    lax.fori_loop(0, rows, drain, 0, unroll=8)


def gather_call(idx, src, rows_per_step):
    n = idx.shape[0]
    d = src.shape[1]
    grid_spec = pltpu.PrefetchScalarGridSpec(
        num_scalar_prefetch=1,
        grid=(n // rows_per_step,),
        in_specs=[pl.BlockSpec(memory_space=pl.ANY)],
        out_specs=pl.BlockSpec((rows_per_step, d), lambda i, idx: (i, 0)),
        scratch_shapes=[pltpu.SemaphoreType.DMA(())],
    )
    return pl.pallas_call(
        _gather_kernel,
        grid_spec=grid_spec,
        out_shape=jax.ShapeDtypeStruct((n, d), src.dtype),
        compiler_params=_cparams(("arbitrary",)),
        name="row_gather",
    )(idx, src)


def _expert_kernel(be_ref, nu_ref, x_ref, wg_ref, wu_ref, wd_ref, o_ref, wg_bf, wu_bf, wd_bf):
    i = pl.program_id(0)
    e = be_ref[i]
    prev = be_ref[jnp.maximum(i - 1, 0)]

    @pl.when((i == 0) | (e != prev))
    def _():
        wg_bf[...] = wg_ref[...].astype(BF16)
        wu_bf[...] = wu_ref[...].astype(BF16)
        wd_bf[...] = wd_ref[...].astype(BF16)

    @pl.when(i < nu_ref[0])
    def _():
        x = x_ref[...].astype(BF16)
        g = jnp.dot(x, wg_bf[...], preferred_element_type=F32)
        u = jnp.dot(x, wu_bf[...], preferred_element_type=F32)
        h = g * jax.nn.sigmoid(g) * u
        o_ref[...] = jnp.dot(h.astype(BF16), wd_bf[...], preferred_element_type=F32)

    @pl.when(i >= nu_ref[0])
    def _():
        o_ref[...] = jnp.zeros(o_ref.shape, o_ref.dtype)


def expert_call(blk_e, n_used, xs, wg, wu, wd, layer):
    n, d = xs.shape
    f = wg.shape[-1]
    nblk = n // MOE_ROWS
    grid_spec = pltpu.PrefetchScalarGridSpec(
        num_scalar_prefetch=2,
        grid=(nblk,),
        in_specs=[
            pl.BlockSpec((MOE_ROWS, d), lambda i, be, nu: (i, 0)),
            pl.BlockSpec((None, None, d, f), lambda i, be, nu: (layer, be[i], 0, 0)),
            pl.BlockSpec((None, None, d, f), lambda i, be, nu: (layer, be[i], 0, 0)),
            pl.BlockSpec((None, None, f, d), lambda i, be, nu: (layer, be[i], 0, 0)),
        ],
        out_specs=pl.BlockSpec((MOE_ROWS, d), lambda i, be, nu: (i, 0)),
        scratch_shapes=[pltpu.VMEM((d, f), BF16), pltpu.VMEM((d, f), BF16), pltpu.VMEM((f, d), BF16)],
    )
    return pl.pallas_call(
        _expert_kernel,
        grid_spec=grid_spec,
        out_shape=jax.ShapeDtypeStruct((n, d), F32),
        compiler_params=_cparams(("arbitrary",)),
        name="experts",
    )(blk_e, n_used, xs, wg, wu, wd)


def _combine_kernel(x1_ref, y0_ref, y1_ref, rt_ref, gt2_ref, fg_ref, o_ref, *, final):
    rt = rt_ref[...]
    x2 = x1_ref[...] + gt2_ref[0] * (rt[:, 2:3] * y0_ref[...] + rt[:, 3:4] * y1_ref[...])
    if final:
        x2 = x2 * lax.rsqrt(jnp.mean(x2 * x2, axis=-1, keepdims=True) + EPS) * fg_ref[...]
    o_ref[...] = x2


def combine_call(x1, ya, rt, gt2, fg, tm, seq_tiles, final):
    t, d = x1.shape
    mod_rows = gt2.shape[1]
    if mod_rows == 1:
        mod_map = lambda i: (i // seq_tiles, 0, 0)
    else:
        mod_map = lambda i: (0, 0, 0)
    nt = t // tm
    return pl.pallas_call(
        functools.partial(_combine_kernel, final=final),
        grid=(nt,),
        in_specs=[
            pl.BlockSpec((tm, d), lambda i: (i, 0)),
            pl.BlockSpec((tm, d), lambda i: (i, 0)),
            pl.BlockSpec((tm, d), lambda i: (i + nt, 0)),
            pl.BlockSpec((tm, LANES), lambda i: (i, 0)),
            pl.BlockSpec((1, mod_rows, d), mod_map),
            pl.BlockSpec((1, d), lambda i: (0, 0)),
        ],
        out_specs=pl.BlockSpec((tm, d), lambda i: (i, 0)),
        out_shape=jax.ShapeDtypeStruct((t, d), F32),
        compiler_params=_cparams(("arbitrary",)),
        name="combine",
    )(x1, ya, ya, rt, gt2, fg)


def moe_block(h2, x1, rt, gt2, fg, wg, wu, wd, layer, tm, seq_tiles, final):
    t, d = h2.shape
    a = t * EXPERT_TOPK
    eid = rt[:, :EXPERT_TOPK].astype(jnp.int32)
    flat = eid.T.reshape(-1)
    onehot = (flat[:, None] == jnp.arange(N_EXPERTS, dtype=jnp.int32)[None, :]).astype(jnp.int32)
    csum = jnp.cumsum(onehot, axis=0)
    rank = jnp.sum(onehot * csum, axis=1) - 1
    counts = csum[-1]
    padded = (counts + MOE_ROWS - 1) // MOE_ROWS * MOE_ROWS
    pad_ends = jnp.cumsum(padded)
    pad_starts = pad_ends - padded
    dest = (pad_starts[flat] + rank).astype(jnp.int32)
    nblk = -(-a // MOE_ROWS) + N_EXPERTS
    tok = jnp.tile(jnp.arange(t, dtype=jnp.int32), EXPERT_TOPK)
    src_tok = jnp.zeros((nblk * MOE_ROWS,), jnp.int32).at[dest].set(tok)
    n_used = (pad_ends[-1] // MOE_ROWS).astype(jnp.int32)
    blk_start = jnp.arange(nblk, dtype=jnp.int32) * MOE_ROWS
    blk_e = jnp.minimum(jnp.searchsorted(pad_ends, blk_start, side='right'), N_EXPERTS - 1).astype(jnp.int32)
    last_e = blk_e[jnp.maximum(n_used - 1, 0)]
    blk_e = jnp.where(jnp.arange(nblk) < n_used, blk_e, last_e)

    xs = gather_call(src_tok, h2, MOE_ROWS)
    yb = expert_call(blk_e, n_used.reshape(1), xs, wg, wu, wd, layer)
    rows = min(MOE_ROWS, a)
    ya = gather_call(dest, yb, rows)
    return combine_call(x1, ya, rt, gt2, fg, tm, seq_tiles, final)


def layer_prompt(x, mods, lw, layer, rope, b, s, lam_init, final, fg):
    t, d = x.shape
    bw = d // 2
    sh1, sc1, gt1, sh2, sc2, gt2 = mods
    tm_in = min(512, s)
    (uv, gates, mk, mv, dk, dv, mqt, dqt, mkb, dkb, mvt, dvt) = inproj_attn_call(
        x, sc1, sh1, lw['n1'], lw['win'], rope, lw['lng'], lw['lnb'], tm_in, b, s)
    ob_t = attn_call(mqt, mkb, mvt, 1, True)
    oc_t = attn_call(dqt, dkb, dvt, 2, False, lw['dlam'], lw['subg_col'], lam_init)
    tm = 256
    x1, h2, rt = merge_call(x, uv, gates, ob_t.reshape(b, bw, s), oc_t.reshape(b, bw, s), gt1, sc2, sh2,
                            lw['n2'], lw['ws'], lw['bs_tab'], lw['wbr'], lw['wo'], lw['wr'], lw['br'],
                            tm, s // tm, 0, 0, True)
    x2 = moe_block(h2, x1, rt, gt2, fg, lw['weg'], lw['weu'], lw['wed'], layer, tm, s // tm, final)
    return x2, (mk, mv, dk, dv)


def layer_sample(x, mods, lw, layer, rope, db, ds, lam_init, final, fg, caches, page_table):
    t, d = x.shape
    bw = d // 2
    sh1, sc1, gt1, sh2, sc2, gt2 = mods
    zt = inproj_call(x, sc1, sh1, lw['n1'], lw['win'], rope, lw['lng'], lw['lnb'], t, 1)
    scale = HEAD_DIM ** -0.5
    eye = jnp.eye(N_UNITS, dtype=F32)

    def qbd(col0):
        q = zt[:, col0:col0 + bw].reshape(db, ds, N_UNITS, HEAD_DIM) * scale
        return jnp.einsum('btud,uw->btuwd', q, eye).reshape(db, ds * N_UNITS, bw)

    def pad_new(col0):
        a = zt[:, col0:col0 + bw].reshape(db, ds, bw)
        return jnp.pad(a, ((0, 0), (0, LANES - ds), (0, 0)))

    unit = np.arange(ds * N_UNITS) % N_UNITS
    col = np.arange(bw)
    om_moba = jnp.asarray((col[None, :] // HEAD_DIM == unit[:, None]).astype(np.float32))
    om_diff = jnp.asarray((col[None, :] // (2 * HEAD_DIM) == unit[:, None] // 2).astype(np.float32))
    ckt, cvt, cdkt, cdv = caches
    ob = sattn_call(page_table, qbd(2 * bw), ckt, cvt, pad_new(3 * bw), pad_new(4 * bw), om_moba,
                    lw['dlam'], lw['subg_row'], layer, True, lam_init, ds).reshape(t, bw)
    oc = sattn_call(page_table, qbd(5 * bw), cdkt, cdv, pad_new(6 * bw), pad_new(7 * bw), om_diff,
                    lw['dlam'], lw['subg_row'], layer, False, lam_init, ds).reshape(t, bw)
    x1, h2, rt = merge_call(x, zt, zt, ob, oc, gt1, sc2, sh2, lw['n2'], lw['ws_s'], lw['bs_tab_s'], lw['wbr'],
                            lw['wo'], lw['wr'], lw['br'], t, 1, 0, (8 * bw) // d, False)
    x2 = moe_block(h2, x1, rt, gt2, fg, lw['weg'], lw['weu'], lw['wed'], layer, t, 1, final)
    outs = (zt[:, 3 * bw:4 * bw], zt[:, 4 * bw:5 * bw], zt[:, 6 * bw:7 * bw], zt[:, 7 * bw:8 * bw],
            zt[:, bw:2 * bw])
    return x2, outs


def kernel(x_prompt, x_sample, cache_moba_k, cache_moba_v, cache_diff_k, cache_diff_v, page_table, c_prompt, c_sample, norm1_g, norm2_g, final_g, w_ada, b_ada, w_in, gm_ln_g, gm_ln_b, gm_ws, gm_bs, diff_lambda, diff_subln_g, w_branch, w_out, w_group, b_group, w_router, b_router, w_gate_e, w_up_e, w_down_e):
    b, s, d = x_prompt.shape
    db, ds, _ = x_sample.shape
    depth = w_in.shape[0]
    bw = d // 2
    gd = bw // GM_GROUPS
    npool, page = cache_moba_k.shape[1], cache_moba_k.shape[2]
    past_len = page_table.shape[1] * page
    ts = db * ds
    assert s % ATT_BLOCK == 0 and ts % GM_CHUNK == 0 and GM_CHUNK % ds == 0

    rope_p = rope_tables(jnp.arange(s, dtype=jnp.int32))
    rope_s = rope_tables(jnp.tile(past_len + jnp.arange(ds, dtype=jnp.int32), db))
    caches = (
        jnp.transpose(cache_moba_k, (0, 1, 3, 4, 2)).reshape(depth, npool, bw, page),
        jnp.transpose(cache_moba_v, (0, 1, 3, 4, 2)).reshape(depth, npool, bw, page),
        jnp.transpose(cache_diff_k, (0, 1, 3, 4, 5, 2)).reshape(depth, npool, bw, page),
        cache_diff_v.reshape(depth, npool, page * cache_diff_v.shape[3], cache_diff_v.shape[4]),
    )

    nc = b + db
    ncp = -(-nc // 8) * 8
    c_all = jnp.pad(jnp.concatenate([c_prompt, c_sample], axis=0), ((0, ncp - nc), (0, 0)))
    tril = jnp.tril(jnp.ones((GM_CHUNK, GM_CHUNK), F32))
    fg = final_g.reshape(1, d)

    xp = x_prompt.reshape(b * s, d)
    xs = x_sample.reshape(ts, d)
    outs_p, outs_s = [], []
    for l in range(depth):
        lam_init = 0.8 - 0.6 * math.exp(-0.3 * l)
        final = l == depth - 1
        m = ada_call(c_all, w_ada, b_ada, l)
        parts = [m[:, i * d:(i + 1) * d] for i in range(6)]
        mods_p = tuple(p[:b].reshape(b, 1, d) for p in parts)
        mods_s = tuple(jnp.repeat(p[b:nc], ds, axis=0).reshape(1, ts, d) for p in parts)

        ws_tril = gm_ws[l] * tril[None]
        ws_small = ws_tril[:, :ds, :ds]
        ws_s = jnp.einsum('ab,gts->gatbs', jnp.eye(GM_CHUNK // ds, dtype=F32), ws_small)
        ws_s = ws_s.reshape(GM_GROUPS, GM_CHUNK, GM_CHUNK)
        bs_tab = jnp.repeat(gm_bs[l].T, gd, axis=1)
        bs_tab_s = jnp.tile(bs_tab[:ds], (GM_CHUNK // ds, 1))
        wr = jnp.pad(jnp.concatenate([w_group[l], w_router[l]], axis=1),
                     ((0, 0), (0, LANES - N_GROUPS - N_EXPERTS)))
        br = jnp.pad(jnp.concatenate([b_group[l], b_router[l]]), (0, LANES - N_GROUPS - N_EXPERTS))
        lw = dict(
            n1=norm1_g[l].reshape(1, d), n2=norm2_g[l].reshape(1, d),
            win=w_in[l].astype(BF16), lng=gm_ln_g[l].reshape(1, bw), lnb=gm_ln_b[l].reshape(1, bw),
            ws=ws_tril.astype(BF16), bs_tab=bs_tab, ws_s=ws_s.astype(BF16), bs_tab_s=bs_tab_s,
            dlam=diff_lambda[l], subg_col=diff_subln_g[l].reshape(-1, 1), subg_row=diff_subln_g[l].reshape(1, -1),
            wbr=w_branch[l].astype(BF16), wo=w_out[l].astype(BF16), wr=wr, br=br.reshape(1, LANES),
            weg=w_gate_e, weu=w_up_e, wed=w_down_e,
        )
        xp, op = layer_prompt(xp, mods_p, lw, l, rope_p, b, s, lam_init, final, fg)
        xs, os_ = layer_sample(xs, mods_s, lw, l, rope_s, db, ds, lam_init, final, fg, caches, page_table)
        outs_p.append(op)
        outs_s.append(os_)

    nh = N_UNITS
    stack = lambda lst, i, shape: jnp.stack([o[i].reshape(shape) for o in lst])
    return (
        xp.reshape(b, s, d), xs.reshape(db, ds, d),
        stack(outs_p, 0, (b, s, nh, HEAD_DIM)), stack(outs_p, 1, (b, s, nh, HEAD_DIM)),
        stack(outs_p, 2, (b, s, nh // 2, 2, HEAD_DIM)), stack(outs_p, 3, (b, s, nh // 2, 2 * HEAD_DIM)),
        stack(outs_s, 0, (db, ds, nh, HEAD_DIM)), stack(outs_s, 1, (db, ds, nh, HEAD_DIM)),
        stack(outs_s, 2, (db, ds, nh // 2, 2, HEAD_DIM)), stack(outs_s, 3, (db, ds, nh // 2, 2 * HEAD_DIM)),
        stack(outs_s, 4, (db, ds, bw)),
    )
```

```python
import functools
import math

import jax
import jax.numpy as jnp
import numpy as np
from jax import lax
from jax.experimental import pallas as pl
from jax.experimental.pallas import tpu as pltpu

F32 = jnp.float32
BF16 = jnp.bfloat16
HIGHEST = lax.Precision.HIGHEST

EPS = 1e-6
NEG = -1e30
BIG = 1e30

N_BRANCHES = 3
GM_GROUPS = 4
GM_CHUNK = 128
N_UNITS = 8
HEAD_DIM = 64
ROT = HEAD_DIM // 4
ROPE_THETA = 500000.0
ATT_BLOCK = 256
ATT_CHAINS = 8
MOBA_TOPK = 3
SAMPLE_BLOCKS_PER_STEP = 4
NEW_ROWS = 16
N_GROUPS = 4
EXPERTS_PER_GROUP = 8
N_EXPERTS = N_GROUPS * EXPERTS_PER_GROUP
EXPERT_TOPK = 2
MOE_ROWS = 256
GATHER_UNROLL = 8
LANES = 128
VMEM_LIMIT = 48 * 1024 * 1024
NT_DIMS = (((1,), (1,)), ((), ()))


def _cparams(sem):
    return pltpu.CompilerParams(dimension_semantics=sem, vmem_limit_bytes=VMEM_LIMIT)


def _ada_kernel(c_ref, w_ref, b_ref, o_ref):
    c = c_ref[...]
    a = c * jax.nn.sigmoid(c)
    o_ref[...] = jnp.dot(a, w_ref[...], preferred_element_type=F32, precision=HIGHEST) + b_ref[...]


def ada_call(c_all, w_ada, b_ada, layer):
    m, d = c_all.shape
    n = w_ada.shape[-1]
    tn = n // 4
    return pl.pallas_call(
        _ada_kernel,
        grid=(n // tn,),
        in_specs=[
            pl.BlockSpec((m, d), lambda j: (0, 0)),
            pl.BlockSpec((None, d, tn), lambda j: (layer, 0, j)),
            pl.BlockSpec((None, 1, tn), lambda j: (layer, 0, j)),
        ],
        out_specs=pl.BlockSpec((m, tn), lambda j: (0, j)),
        out_shape=jax.ShapeDtypeStruct((m, n), F32),
        compiler_params=_cparams(("arbitrary",)),
        name="ada",
    )(c_all, w_ada, b_ada.reshape(b_ada.shape[0], 1, n))


def _norm_mod(x_ref, sc_ref, sh_ref, g_ref):
    x = x_ref[...]
    y = x * lax.rsqrt(jnp.mean(x * x, axis=-1, keepdims=True) + EPS) * g_ref[...]
    return (y * (1.0 + sc_ref[0]) + sh_ref[0]).astype(BF16)


def _gelu_ln(z, lng_ref, lnb_ref):
    a = jax.nn.gelu(z)
    mu = jnp.mean(a, axis=-1, keepdims=True)
    ac = a - mu
    var = jnp.mean(ac * ac, axis=-1, keepdims=True)
    return ac * lax.rsqrt(var + EPS) * lng_ref[...] + lnb_ref[...]


def _rope(z, rc_ref, rs1_ref, rs2_ref):
    width = z.shape[-1]
    reps = width // LANES
    c = jnp.concatenate([rc_ref[...]] * reps, axis=1)
    s1 = jnp.concatenate([rs1_ref[...]] * reps, axis=1)
    s2 = jnp.concatenate([rs2_ref[...]] * reps, axis=1)
    up = pltpu.roll(z, width - ROT // 2, 1)
    dn = pltpu.roll(z, ROT // 2, 1)
    return z * c + up * s1 + dn * s2


def _inproj_kernel(x_ref, sc_ref, sh_ref, g_ref, w_ref, rc_ref, rs1_ref, rs2_ref, lng_ref, lnb_ref,
                   o_ref, h_scr):
    j = pl.program_id(1)

    @pl.when(j == 0)
    def _():
        h_scr[...] = _norm_mod(x_ref, sc_ref, sh_ref, g_ref)

    z = jnp.dot(h_scr[...], w_ref[...], preferred_element_type=F32)

    @pl.when(j == 0)
    def _():
        o_ref[...] = jax.nn.gelu(z)

    @pl.when(j == 1)
    def _():
        o_ref[...] = _gelu_ln(z, lng_ref, lnb_ref)

    @pl.when((j == 2) | (j == 3) | (j == 5) | (j == 6))
    def _():
        o_ref[...] = _rope(z, rc_ref, rs1_ref, rs2_ref)

    @pl.when((j == 4) | (j == 7))
    def _():
        o_ref[...] = z

    @pl.when(j >= 8)
    def _():
        o_ref[...] = jax.nn.sigmoid(z)


def _inproj_attn_kernel(x_ref, sc_ref, sh_ref, g_ref, w_ref, rc_ref, rs1_ref, rs2_ref, lng_ref, lnb_ref,
                        uv_ref, gates_ref, mk_ref, mv_ref, dk_ref, dv_ref,
                        mqt_ref, dqt_ref, mkb_ref, dkb_ref, mvt_ref, dvt_ref, h_scr):
    j = pl.program_id(1)
    scale = HEAD_DIM ** -0.5 * math.log2(math.e)

    @pl.when(j == 0)
    def _():
        h_scr[...] = _norm_mod(x_ref, sc_ref, sh_ref, g_ref)

    z = jnp.dot(h_scr[...], w_ref[...], preferred_element_type=F32)

    def put_qt(ref, zr):
        zt = (zr * scale).T.astype(BF16)
        for u in range(ref.shape[0]):
            ref[u] = zt[u * HEAD_DIM:(u + 1) * HEAD_DIM, :]

    def put_kb(ref, zr):
        zb = zr.astype(BF16)
        for u in range(ref.shape[0]):
            for kb in range(ref.shape[1]):
                ref[u, kb] = zb[kb * ATT_BLOCK:(kb + 1) * ATT_BLOCK, u * HEAD_DIM:(u + 1) * HEAD_DIM]

    def put_vt(ref, zv):
        zt = zv.T.astype(BF16)
        dv = ref.shape[2]
        for n in range(ref.shape[0]):
            for kb in range(ref.shape[1]):
                ref[n, kb] = zt[n * dv:(n + 1) * dv, kb * ATT_BLOCK:(kb + 1) * ATT_BLOCK]

    @pl.when(j == 0)
    def _():
        uv_ref[...] = jax.nn.gelu(z)

    @pl.when(j == 1)
    def _():
        uv_ref[...] = _gelu_ln(z, lng_ref, lnb_ref)

    @pl.when(j == 2)
    def _():
        put_qt(mqt_ref, _rope(z, rc_ref, rs1_ref, rs2_ref))

    @pl.when(j == 3)
    def _():
        zr = _rope(z, rc_ref, rs1_ref, rs2_ref)
        mk_ref[...] = zr
        put_kb(mkb_ref, zr)

    @pl.when(j == 4)
    def _():
        mv_ref[...] = z
        put_vt(mvt_ref, z)

    @pl.when(j == 5)
    def _():
        put_qt(dqt_ref, _rope(z, rc_ref, rs1_ref, rs2_ref))

    @pl.when(j == 6)
    def _():
        zr = _rope(z, rc_ref, rs1_ref, rs2_ref)
        dk_ref[...] = zr
        put_kb(dkb_ref, zr)

    @pl.when(j == 7)
    def _():
        dv_ref[...] = z
        put_vt(dvt_ref, z)

    @pl.when(j >= 8)
    def _():
        gates_ref[...] = jax.nn.sigmoid(z)


def _inproj_in_specs(tm, d, tn, mod_rows, seq_tiles):
    if mod_rows == 1:
        mod_map = lambda i, j: (i // seq_tiles, 0, 0)
    else:
        mod_map = lambda i, j: (0, 0, 0)
    rope_map = lambda i, j: (i % seq_tiles, 0)
    return [
        pl.BlockSpec((tm, d), lambda i, j: (i, 0)),
        pl.BlockSpec((1, mod_rows, d), mod_map),
        pl.BlockSpec((1, mod_rows, d), mod_map),
        pl.BlockSpec((1, d), lambda i, j: (0, 0)),
        pl.BlockSpec((d, tn), lambda i, j: (0, j)),
        pl.BlockSpec((tm, LANES), rope_map),
        pl.BlockSpec((tm, LANES), rope_map),
        pl.BlockSpec((tm, LANES), rope_map),
        pl.BlockSpec((1, tn), lambda i, j: (0, 0)),
        pl.BlockSpec((1, tn), lambda i, j: (0, 0)),
    ]


def inproj_call(x, sc, sh, g, w_bf, rope, lng, lnb, tm, seq_tiles):
    t, d = x.shape
    n = w_bf.shape[1]
    tn = 512
    rc, rs1, rs2 = rope
    return pl.pallas_call(
        _inproj_kernel,
        grid=(t // tm, n // tn),
        in_specs=_inproj_in_specs(tm, d, tn, sc.shape[1], seq_tiles),
        out_specs=pl.BlockSpec((tm, tn), lambda i, j: (i, j)),
        out_shape=jax.ShapeDtypeStruct((t, n), F32),
        scratch_shapes=[pltpu.VMEM((tm, d), BF16)],
        compiler_params=_cparams(("arbitrary", "arbitrary")),
        name="inproj",
    )(x, sc, sh, g, w_bf, rc, rs1, rs2, lng, lnb)


def inproj_attn_call(x, sc, sh, g, w_bf, rope, lng, lnb, tm, b, s):
    t, d = x.shape
    tn = 512
    n = w_bf.shape[1]
    seq_tiles = s // tm
    nb = s // ATT_BLOCK
    kbt = tm // ATT_BLOCK
    nu, dh = N_UNITS, HEAD_DIM
    rc, rs1, rs2 = rope
    bi = lambda i: i // seq_tiles
    si = lambda i: i % seq_tiles
    row_spec = lambda w: pl.BlockSpec((tm, w), lambda i, j: (i, 0))
    qt_spec = pl.BlockSpec((None, nu, dh, tm), lambda i, j: (bi(i), 0, 0, si(i)))
    kb_spec = pl.BlockSpec((None, nu, kbt, ATT_BLOCK, dh), lambda i, j: (bi(i), 0, si(i), 0, 0))

    def vt_spec(nv):
        return pl.BlockSpec((None, nv, kbt, tn // nv, ATT_BLOCK), lambda i, j: (bi(i), 0, si(i), 0, 0))

    sds = jax.ShapeDtypeStruct
    return pl.pallas_call(
        _inproj_attn_kernel,
        grid=(t // tm, n // tn),
        in_specs=_inproj_in_specs(tm, d, tn, sc.shape[1], seq_tiles),
        out_specs=[
            pl.BlockSpec((tm, tn), lambda i, j: (i, jnp.minimum(j, 1))),
            pl.BlockSpec((tm, tn), lambda i, j: (i, jnp.maximum(j - 8, 0))),
            row_spec(tn), row_spec(tn), row_spec(tn), row_spec(tn),
            qt_spec, qt_spec, kb_spec, kb_spec, vt_spec(nu), vt_spec(nu // 2),
        ],
        out_shape=[
            sds((t, 2 * tn), F32), sds((t, n - 8 * tn), F32),
            sds((t, tn), F32), sds((t, tn), F32), sds((t, tn), F32), sds((t, tn), F32),
            sds((b, nu, dh, s), BF16), sds((b, nu, dh, s), BF16),
            sds((b, nu, nb, ATT_BLOCK, dh), BF16), sds((b, nu, nb, ATT_BLOCK, dh), BF16),
            sds((b, nu, nb, dh, ATT_BLOCK), BF16), sds((b, nu // 2, nb, 2 * dh, ATT_BLOCK), BF16),
        ],
        scratch_shapes=[pltpu.VMEM((tm, d), BF16)],
        compiler_params=_cparams(("arbitrary", "arbitrary")),
        name="inproj_attn",
    )(x, sc, sh, g, w_bf, rc, rs1, rs2, lng, lnb)


def rope_tables(pos):
    half = ROT // 2
    inv = ROPE_THETA ** (-(jnp.arange(half, dtype=F32) * 2.0 / ROT))
    ang = pos.astype(F32)[:, None] * inv[None, :]
    cos, sin = jnp.cos(ang), jnp.sin(ang)
    ones = jnp.ones((pos.shape[0], HEAD_DIM - ROT), F32)
    zeros8 = jnp.zeros((pos.shape[0], half), F32)
    zrest = jnp.zeros_like(ones)
    c = jnp.concatenate([cos, cos, ones], axis=1)
    s1 = jnp.concatenate([-sin, zeros8, zrest], axis=1)
    s2 = jnp.concatenate([zeros8, sin, zrest], axis=1)
    tile = lambda a: jnp.concatenate([a] * (LANES // HEAD_DIM), axis=1)
    return tile(c), tile(s1), tile(s2)


def _diff_lambda(dl_ref, lam_init):
    dl = dl_ref[...]
    return (jnp.exp(jnp.sum(dl[0:1] * dl[1:2], axis=1, keepdims=True))
            - jnp.exp(jnp.sum(dl[2:3] * dl[3:4], axis=1, keepdims=True)) + lam_init)


def _attn_kernel(*refs, nmaps, use_sel, lam_init):
    if use_sel:
        q_ref, k_ref, v_ref, o_ref, kmean_scr, sel_scr = refs
    else:
        q_ref, k_ref, v_ref, dl_ref, sg_ref, o_ref = refs
    qi = pl.program_id(2)
    nchain = q_ref.shape[0]
    nb = k_ref.shape[1]
    tq = q_ref.shape[-1]
    blk = k_ref.shape[2]

    if use_sel:
        @pl.when(qi == 0)
        def _():
            for c in range(nchain):
                kmean_scr[c] = jnp.mean(k_ref[c].astype(F32), axis=1)

        row = lax.broadcasted_iota(jnp.int32, (nb, tq), 0)
        for c in range(nchain):
            gate = jnp.dot(kmean_scr[c], q_ref[c].astype(F32), preferred_element_type=F32,
                           precision=HIGHEST)
            past = row < qi
            gate = jnp.where(past, gate, NEG)
            sel = jnp.zeros((nb, tq), F32)
            for _ in range(MOBA_TOPK):
                top = jnp.max(gate, axis=0, keepdims=True)
                first = jnp.min(jnp.where(gate == top, row, nb), axis=0, keepdims=True)
                hit = row == first
                sel = jnp.where(hit & past, 1.0, sel)
                gate = jnp.where(hit, -jnp.inf, gate)
            sel_scr[c] = sel

    kr = lax.broadcasted_iota(jnp.int32, (blk, tq), 0)
    qc = lax.broadcasted_iota(jnp.int32, (blk, tq), 1)
    qts = [q_ref[c] for c in range(nchain)]
    ss = [jnp.dot(k_ref[c, qi], qts[c], preferred_element_type=F32) for c in range(nchain)]
    stats = []
    for c in range(nchain):
        s = jnp.where(kr <= qc, ss[c], NEG)
        m0 = jnp.max(s, axis=0, keepdims=True)
        p = jnp.exp2(s - m0)
        stats.append((m0, jnp.sum(p, axis=0, keepdims=True), p.astype(BF16)))
    carry0 = tuple(
        (stats[c][0], stats[c][1],
         jnp.dot(v_ref[c // nmaps, qi], stats[c][2], preferred_element_type=F32))
        for c in range(nchain))

    def kv_body(kj, carry):
        ss = [jnp.dot(k_ref[c, kj], qts[c], preferred_element_type=F32) for c in range(nchain)]
        upd = []
        for c in range(nchain):
            m, l, _ = carry[c]
            smax = jnp.max(ss[c], axis=0, keepdims=True)
            if use_sel:
                on = sel_scr[c, pl.ds(kj, 1), :] > 0.5
                m_new = jnp.where(on, jnp.maximum(m, smax), m)
                m_use = jnp.where(on, m_new, BIG)
            else:
                m_new = jnp.maximum(m, smax)
                m_use = m_new
            p = jnp.exp2(ss[c] - m_use)
            alpha = jnp.exp2(m - m_new)
            upd.append((m_new, alpha * l + jnp.sum(p, axis=0, keepdims=True), alpha, p.astype(BF16)))
        return tuple(
            (upd[c][0], upd[c][1],
             upd[c][2] * carry[c][2] + jnp.dot(v_ref[c // nmaps, kj], upd[c][3], preferred_element_type=F32))
            for c in range(nchain))

    carry = lax.fori_loop(0, qi, kv_body, tuple(carry0))
    outs = [acc / l for (_, l, acc) in carry]

    if use_sel:
        for c in range(nchain):
            o_ref[c] = outs[c]
    else:
        lam = _diff_lambda(dl_ref, lam_init)
        for hh in range(nchain // 2):
            o = outs[2 * hh] - lam * outs[2 * hh + 1]
            ms = jnp.mean(o * o, axis=0, keepdims=True)
            o_ref[hh] = o * lax.rsqrt(ms + EPS) * sg_ref[...] * (1.0 - lam_init)


def attn_call(qt, k, vt, nmaps, use_sel, dlam=None, subg=None, lam_init=0.0):
    b, nu, dh, s = qt.shape
    nb, blk = k.shape[2], k.shape[3]
    nv, dv = vt.shape[1], vt.shape[3]
    tq = blk
    nc = ATT_CHAINS
    nvb = nc // nmaps
    kern = functools.partial(_attn_kernel, nmaps=nmaps, use_sel=use_sel, lam_init=lam_init)
    in_specs = [
        pl.BlockSpec((None, nc, dh, tq), lambda bi, n, qi: (bi, n, 0, qi)),
        pl.BlockSpec((None, nc, nb, blk, dh), lambda bi, n, qi: (bi, n, 0, 0, 0)),
        pl.BlockSpec((None, nvb, nb, dv, blk), lambda bi, n, qi: (bi, n, 0, 0, 0)),
    ]
    args = [qt, k, vt]
    scratch = []
    if use_sel:
        scratch = [pltpu.VMEM((nc, nb, dh), F32), pltpu.VMEM((nc, nb, tq), F32)]
    else:
        in_specs += [pl.BlockSpec(dlam.shape, lambda bi, n, qi: (0, 0)),
                     pl.BlockSpec(subg.shape, lambda bi, n, qi: (0, 0))]
        args += [dlam, subg]
    return pl.pallas_call(
        kern,
        grid=(b, nu // nc, s // tq),
        in_specs=in_specs,
        out_specs=pl.BlockSpec((None, nvb, dv, tq), lambda bi, n, qi: (bi, n, 0, qi)),
        out_shape=jax.ShapeDtypeStruct((b, nv, dv, s), F32),
        scratch_shapes=scratch,
        compiler_params=_cparams(("arbitrary", "arbitrary", "arbitrary")),
        name="moba_attn" if use_sel else "diff_attn",
    )(*args)


def _sattn_kernel(pt_ref, *refs, use_sel, nblk, bps, n_new, lam_init):
    del pt_ref
    npg = 2 * bps
    qbd_ref = refs[0]
    kt_refs = refs[1:1 + npg]
    v_refs = refs[1 + npg:1 + 2 * npg]
    kn_ref, vn_ref, om_ref, dl_ref, sg_ref, o_ref, opart, g_scr, m_scr, l_scr = refs[1 + 2 * npg:]
    j = pl.program_id(1)
    qf = qbd_ref[0]
    rows, wdt = qf.shape
    q_hi = qf.astype(BF16)
    q_lo = (qf - q_hi.astype(F32)).astype(BF16)
    lhs = jnp.concatenate([q_hi, q_lo], axis=0)
    lane = lax.broadcasted_iota(jnp.int32, (rows, LANES), 1)
    om = om_ref[...]

    @pl.when(j == 0)
    def _():
        g_scr[...] = jnp.full((rows, LANES), NEG, F32)
        m_scr[...] = jnp.full((rows, LANES), NEG, F32)
        l_scr[...] = jnp.zeros((rows, LANES), F32)

    def scores(kt):
        s2 = jnp.dot(lhs, kt, preferred_element_type=F32)
        return s2[:rows] + s2[rows:]

    gacc, macc, lacc = g_scr[...], m_scr[...], l_scr[...]
    ss = [jnp.concatenate([scores(kt_refs[2 * bb + w][...].astype(BF16)) for w in range(2)], axis=1)
          for bb in range(bps)]
    stats = []
    for bb in range(bps):
        mj = jnp.max(ss[bb], axis=-1, keepdims=True)
        p = jnp.exp(ss[bb] - mj)
        stats.append((mj, jnp.sum(p, axis=-1, keepdims=True), p.astype(BF16)))
    for bb in range(bps):
        blk_id = j * bps + bb
        s = ss[bb]
        mj, lj, pb = stats[bb]
        oj = None
        for w in range(2):
            pw = pb[:, w * LANES:(w + 1) * LANES]
            vref = v_refs[2 * bb + w]
            if use_sel:
                ow = lax.dot_general(pw, vref[...].astype(BF16), NT_DIMS, preferred_element_type=F32)
            else:
                nh = vref.shape[0] // LANES
                ow = jnp.concatenate(
                    [jnp.dot(pw, vref[pl.ds(hh, LANES, stride=nh), :].astype(BF16),
                             preferred_element_type=F32) for hh in range(nh)], axis=1)
            oj = ow if oj is None else oj + ow
        opart[blk_id] = oj * om
        hit = lane == blk_id
        if use_sel:
            gacc = jnp.where(hit, jnp.sum(s, axis=-1, keepdims=True), gacc)
        macc = jnp.where(hit, mj, macc)
        lacc = jnp.where(hit, lj, lacc)
    g_scr[...] = gacc
    m_scr[...] = macc
    l_scr[...] = lacc

    @pl.when(j == nblk // bps - 1)
    def _():
        zpad = jnp.zeros((LANES - NEW_ROWS, wdt), BF16)
        kn = jnp.concatenate([kn_ref[0].astype(BF16), zpad], axis=0)
        vn = jnp.concatenate([vn_ref[0].astype(BF16), zpad], axis=0)
        so2 = lax.dot_general(lhs, kn, NT_DIMS, preferred_element_type=F32)
        so = so2[:rows] + so2[rows:]
        rtok = lax.broadcasted_iota(jnp.int32, (rows, LANES), 0) // N_UNITS
        so = jnp.where((lane <= rtok) & (lane < n_new), so, NEG)
        mo = jnp.max(so, axis=-1, keepdims=True)
        po = jnp.exp(so - mo)
        lo = jnp.sum(po, axis=-1, keepdims=True)
        oo = jnp.dot(po.astype(BF16), vn, preferred_element_type=F32) * om

        inb = lane < nblk
        if use_sel:
            g = jnp.where(inb, gacc, -jnp.inf)
            selm = jnp.zeros((rows, LANES), F32)
            for _ in range(min(MOBA_TOPK, nblk)):
                top = jnp.max(g, axis=-1, keepdims=True)
                first = jnp.min(jnp.where(g == top, lane, LANES), axis=-1, keepdims=True)
                hit = lane == first
                selm = jnp.where(hit, 1.0, selm)
                g = jnp.where(hit, -jnp.inf, g)
            selb = selm > 0.5
        else:
            selb = inb
        mt = jnp.maximum(jnp.max(jnp.where(selb, macc, NEG), axis=-1, keepdims=True), mo)
        w = jnp.where(selb, jnp.exp(macc - mt), 0.0)
        wo = jnp.exp(mo - mt)
        ltot = jnp.sum(w * lacc, axis=-1, keepdims=True) + wo * lo
        acc = wo * oo
        for jb in range(nblk):
            acc = acc + w[:, jb:jb + 1] * opart[jb]
        res = acc / ltot
        if not use_sel:
            lam = _diff_lambda(dl_ref, lam_init)
            rr = lax.broadcasted_iota(jnp.int32, (rows, 1), 0)
            res = res * jnp.where(rr % 2 == 0, 1.0, -lam)
        out = jnp.sum(res.reshape(rows // N_UNITS, N_UNITS, wdt), axis=1)
        if not use_sel:
            dv = sg_ref.shape[-1]
            segs = []
            for hh in range(wdt // dv):
                seg = out[:, hh * dv:(hh + 1) * dv]
                ms = jnp.mean(seg * seg, axis=-1, keepdims=True)
                segs.append(seg * lax.rsqrt(ms + EPS) * sg_ref[...] * (1.0 - lam_init))
            out = jnp.concatenate(segs, axis=1)
        o_ref[0] = out


def sattn_call(page_table, qbd, cache_kt, cache_v, knew, vnew, omask, dlam, subg, layer, use_sel,
               lam_init, n_new):
    db, rows, w = qbd.shape
    page = cache_kt.shape[3]
    n_pages = page_table.shape[1]
    nblk = n_pages * page // ATT_BLOCK
    bps = SAMPLE_BLOCKS_PER_STEP
    assert ATT_BLOCK == 2 * page and page == LANES and nblk <= LANES and nblk % bps == 0
    pt_flat = page_table.reshape(-1)

    def page_spec(arr, which):
        return pl.BlockSpec((None, None) + arr.shape[2:],
                            lambda b, j, pt: (layer, pt[b * n_pages + 2 * bps * j + which], 0, 0))

    kern = functools.partial(_sattn_kernel, use_sel=use_sel, nblk=nblk, bps=bps, n_new=n_new,
                             lam_init=lam_init)
    n_tok = rows // N_UNITS
    grid_spec = pltpu.PrefetchScalarGridSpec(
        num_scalar_prefetch=1,
        grid=(db, nblk // bps),
        in_specs=(
            [pl.BlockSpec((1, rows, w), lambda b, j, pt: (b, 0, 0))]
            + [page_spec(cache_kt, i) for i in range(2 * bps)]
            + [page_spec(cache_v, i) for i in range(2 * bps)]
            + [pl.BlockSpec((1, NEW_ROWS, w), lambda b, j, pt: (b, 0, 0)),
               pl.BlockSpec((1, NEW_ROWS, w), lambda b, j, pt: (b, 0, 0)),
               pl.BlockSpec((rows, w), lambda b, j, pt: (0, 0)),
               pl.BlockSpec(dlam.shape, lambda b, j, pt: (0, 0)),
               pl.BlockSpec(subg.shape, lambda b, j, pt: (0, 0))]
        ),
        out_specs=pl.BlockSpec((1, n_tok, w), lambda b, j, pt: (b, 0, 0)),
        scratch_shapes=[pltpu.VMEM((nblk, rows, w), F32), pltpu.VMEM((rows, LANES), F32),
                        pltpu.VMEM((rows, LANES), F32), pltpu.VMEM((rows, LANES), F32)],
    )
    return pl.pallas_call(
        kern,
        grid_spec=grid_spec,
        out_shape=jax.ShapeDtypeStruct((db, n_tok, w), F32),
        compiler_params=_cparams(("arbitrary", "arbitrary")),
        name="moba_sample" if use_sel else "diff_sample",
    )(pt_flat, qbd, *([cache_kt] * (2 * bps)), *([cache_v] * (2 * bps)), knew, vnew, omask, dlam, subg)


def _merge_kernel(x_ref, u_ref, v_ref, ob_ref, oc_ref, ga_ref, gb_ref, gc_ref, gt1_ref, sc2_ref,
                  sh2_ref, n2_ref, ws_ref, bs_ref, wbr_ref, wo_ref, wr_ref, br_ref,
                  x1_ref, h2_ref, rt_ref, cnt_ref, *, o_transposed):
    tm = x_ref.shape[0]
    gd = ws_ref.shape[-1]
    v = v_ref[...].astype(BF16)
    chunks = []
    for c in range(tm // GM_CHUNK):
        r0 = c * GM_CHUNK
        cols = [jnp.dot(ws_ref[g], v[r0:r0 + GM_CHUNK, g * gd:(g + 1) * gd], preferred_element_type=F32)
                for g in range(GM_GROUPS)]
        chunks.append(jnp.concatenate(cols, axis=1) + bs_ref[...])
    mixed = jnp.concatenate(chunks, axis=0) if len(chunks) > 1 else chunks[0]
    o_a = u_ref[...] * mixed
    if o_transposed:
        o_b = ob_ref[...].T
        o_c = oc_ref[...].T
    else:
        o_b = ob_ref[...]
        o_c = oc_ref[...]
    merged = (ga_ref[...] * jnp.dot(o_a.astype(BF16), wbr_ref[0], preferred_element_type=F32)
              + gb_ref[...] * jnp.dot(o_b.astype(BF16), wbr_ref[1], preferred_element_type=F32)
              + gc_ref[...] * jnp.dot(o_c.astype(BF16), wbr_ref[2], preferred_element_type=F32))
    y = jnp.dot(merged.astype(BF16), wo_ref[...], preferred_element_type=F32)
    x1 = x_ref[...] + gt1_ref[0] * y
    x1_ref[...] = x1
    h2 = x1 * lax.rsqrt(jnp.mean(x1 * x1, axis=-1, keepdims=True) + EPS) * n2_ref[...]
    h2 = h2 * (1.0 + sc2_ref[0]) + sh2_ref[0]
    h2_ref[...] = h2

    h_hi = h2.astype(BF16)
    h_lo = (h2 - h_hi.astype(F32)).astype(BF16)
    logits = (jnp.dot(h_hi, wr_ref[0], preferred_element_type=F32)
              + jnp.dot(h_lo, wr_ref[0], preferred_element_type=F32)
              + jnp.dot(h_hi, wr_ref[1], preferred_element_type=F32)) + br_ref[...]
    lane = lax.broadcasted_iota(jnp.int32, logits.shape, 1)
    gl = jnp.where(lane < N_GROUPS, logits, NEG)
    gmax = jnp.max(gl, axis=-1, keepdims=True)
    gsel = jnp.min(jnp.where(gl == gmax, lane, LANES), axis=-1, keepdims=True)
    gw = 1.0 / jnp.sum(jnp.exp(gl - gmax), axis=-1, keepdims=True)
    lo = N_GROUPS + EXPERTS_PER_GROUP * gsel
    el = jnp.where((lane >= lo) & (lane < lo + EXPERTS_PER_GROUP), logits, NEG)
    v1 = jnp.max(el, axis=-1, keepdims=True)
    i1 = jnp.min(jnp.where(el == v1, lane, LANES), axis=-1, keepdims=True)
    el2 = jnp.where(lane == i1, NEG, el)
    v2 = jnp.max(el2, axis=-1, keepdims=True)
    i2 = jnp.min(jnp.where(el2 == v2, lane, LANES), axis=-1, keepdims=True)
    e2 = jnp.exp(v2 - v1)
    w1 = gw / (1.0 + e2)
    w2 = gw * e2 / (1.0 + e2)

    @pl.when(pl.program_id(0) == 0)
    def _():
        cnt_ref[...] = jnp.zeros(cnt_ref.shape, F32)

    e1 = i1 - N_GROUPS
    e2i = i2 - N_GROUPS
    onehot = jnp.where((lane == e1) | (lane == e2i), 1.0, 0.0)
    rr = lax.broadcasted_iota(jnp.int32, (tm, tm), 0)
    cc = lax.broadcasted_iota(jnp.int32, (tm, tm), 1)
    lower = jnp.where(rr > cc, 1.0, 0.0).astype(BF16)
    before = jnp.dot(lower, onehot.astype(BF16), preferred_element_type=F32) + cnt_ref[0:1, :]
    r1 = jnp.sum(jnp.where(lane == e1, before, 0.0), axis=-1, keepdims=True)
    r2 = jnp.sum(jnp.where(lane == e2i, before, 0.0), axis=-1, keepdims=True)
    cnt_ref[...] = cnt_ref[...] + jnp.sum(onehot, axis=0, keepdims=True)

    vals = (e1.astype(F32), e2i.astype(F32), w1, w2, r1, r2)
    rt = jnp.zeros(logits.shape, F32)
    for k, val in enumerate(vals):
        rt = jnp.where(lane == k, val, rt)
    rt_ref[...] = rt


def merge_call(x, uv, gates, ob, oc, gt1, sc2, sh2, n2, ws_bf, bs_tab, wbr_bf, wo_bf, wr, br, tm,
               seq_tiles, uv_col0, gate_col0, o_transposed):
    t, d = x.shape
    bw = d // 2
    mod_rows = gt1.shape[1]
    if mod_rows == 1:
        mod_map = lambda i: (i // seq_tiles, 0, 0)
    else:
        mod_map = lambda i: (0, 0, 0)
    if o_transposed:
        o_spec = pl.BlockSpec((None, bw, tm), lambda i: (i // seq_tiles, 0, i % seq_tiles))
    else:
        o_spec = pl.BlockSpec((tm, bw), lambda i: (i, 0))
    full = lambda a: pl.BlockSpec(a.shape, lambda i: (0,) * a.ndim)
    return pl.pallas_call(
        functools.partial(_merge_kernel, o_transposed=o_transposed),
        grid=(t // tm,),
        in_specs=[
            pl.BlockSpec((tm, d), lambda i: (i, 0)),
            pl.BlockSpec((tm, bw), lambda i: (i, uv_col0)),
            pl.BlockSpec((tm, bw), lambda i: (i, uv_col0 + 1)),
            o_spec, o_spec,
            pl.BlockSpec((tm, d), lambda i: (i, gate_col0)),
            pl.BlockSpec((tm, d), lambda i: (i, gate_col0 + 1)),
            pl.BlockSpec((tm, d), lambda i: (i, gate_col0 + 2)),
            pl.BlockSpec((1, mod_rows, d), mod_map),
            pl.BlockSpec((1, mod_rows, d), mod_map),
            pl.BlockSpec((1, mod_rows, d), mod_map),
            full(n2), full(ws_bf), full(bs_tab), full(wbr_bf), full(wo_bf), full(wr), full(br),
        ],
        out_specs=[
            pl.BlockSpec((tm, d), lambda i: (i, 0)),
            pl.BlockSpec((tm, d), lambda i: (i, 0)),
            pl.BlockSpec((tm, LANES), lambda i: (i, 0)),
            pl.BlockSpec((8, LANES), lambda i: (0, 0)),
        ],
        out_shape=[jax.ShapeDtypeStruct((t, d), F32), jax.ShapeDtypeStruct((t, d), F32),
                   jax.ShapeDtypeStruct((t, LANES), F32), jax.ShapeDtypeStruct((8, LANES), F32)],
        compiler_params=_cparams(("arbitrary",)),
        name="merge",
    )(x, uv, uv, ob, oc, gates, gates, gates, gt1, sc2, sh2, n2, ws_bf, bs_tab, wbr_bf, wo_bf, wr, br)


def _gather_kernel(idx_ref, src_ref, o_ref, sem):
    rows = o_ref.shape[0]
    base = pl.program_id(0) * rows

    def row_copy(r, tok):
        return pltpu.make_async_copy(src_ref.at[pl.ds(tok, 1)], o_ref.at[pl.ds(r, 1)], sem)

    def issue(g, carry):
        for k in range(GATHER_UNROLL):
            r = g * GATHER_UNROLL + k
            row_copy(r, idx_ref[base + r]).start(priority=k % 2)
        return carry

    lax.fori_loop(0, rows // GATHER_UNROLL, issue, 0)

    def drain(r, carry):
        row_copy(r, 0).wait()
        return carry

    lax.fori_loop(0, rows, drain, 0, unroll=8)


def gather_call(idx, src, rows_per_step):
    n = idx.shape[0]
    d = src.shape[1]
    grid_spec = pltpu.PrefetchScalarGridSpec(
        num_scalar_prefetch=1,
        grid=(n // rows_per_step,),
        in_specs=[pl.BlockSpec(memory_space=pl.ANY)],
        out_specs=pl.BlockSpec((rows_per_step, d), lambda i, idx: (i, 0)),
        scratch_shapes=[pltpu.SemaphoreType.DMA(())],
    )
    return pl.pallas_call(
        _gather_kernel,
        grid_spec=grid_spec,
        out_shape=jax.ShapeDtypeStruct((n, d), src.dtype),
        compiler_params=_cparams(("arbitrary",)),
        name="row_gather",
    )(idx, src)


def _scatter_kernel(dest_ref, h_ref, init_ref, o_ref, sem):
    del init_ref
    tm = h_ref.shape[0]
    base = pl.program_id(0) * tm
    t_total = pl.num_programs(0) * tm

    def row_copy(r, dst):
        return pltpu.make_async_copy(h_ref.at[pl.ds(r, 1)], o_ref.at[pl.ds(dst, 1)], sem)

    def issue(g, carry):
        for k in range(GATHER_UNROLL):
            r = g * GATHER_UNROLL + k
            for slot in range(EXPERT_TOPK):
                row_copy(r, dest_ref[slot * t_total + base + r]).start(priority=slot)
        return carry

    lax.fori_loop(0, tm // GATHER_UNROLL, issue, 0)

    def drain(r, carry):
        row_copy(0, 0).wait()
        return carry

    lax.fori_loop(0, tm * EXPERT_TOPK, drain, 0, unroll=8)


def scatter_call(dest, h2, init, tm):
    t, d = h2.shape
    grid_spec = pltpu.PrefetchScalarGridSpec(
        num_scalar_prefetch=1,
        grid=(t // tm,),
        in_specs=[pl.BlockSpec((tm, d), lambda i, dest: (i, 0)), pl.BlockSpec(memory_space=pl.ANY)],
        out_specs=pl.BlockSpec(memory_space=pl.ANY),
        scratch_shapes=[pltpu.SemaphoreType.DMA(())],
    )
    return pl.pallas_call(
        _scatter_kernel,
        grid_spec=grid_spec,
        out_shape=jax.ShapeDtypeStruct(init.shape, init.dtype),
        input_output_aliases={2: 0},
        compiler_params=_cparams(("arbitrary",)),
        name="row_scatter",
    )(dest, h2, init)


def _expert_kernel(be_ref, nu_ref, x_ref, wg_ref, wu_ref, wd_ref, o_ref, wg_bf, wu_bf, wd_bf):
    i = pl.program_id(0)
    e = be_ref[i]
    prev = be_ref[jnp.maximum(i - 1, 0)]

    @pl.when((i == 0) | (e != prev))
    def _():
        wg_bf[...] = wg_ref[...].astype(BF16)
        wu_bf[...] = wu_ref[...].astype(BF16)
        wd_bf[...] = wd_ref[...].astype(BF16)

    @pl.when(i < nu_ref[0])
    def _():
        x = x_ref[...].astype(BF16)
        g = jnp.dot(x, wg_bf[...], preferred_element_type=F32)
        u = jnp.dot(x, wu_bf[...], preferred_element_type=F32)
        h = g * jax.nn.sigmoid(g) * u
        o_ref[...] = jnp.dot(h.astype(BF16), wd_bf[...], preferred_element_type=F32)

    @pl.when(i >= nu_ref[0])
    def _():
        o_ref[...] = jnp.zeros(o_ref.shape, o_ref.dtype)


def expert_call(blk_e, n_used, xs, wg, wu, wd, layer):
    n, d = xs.shape
    f = wg.shape[-1]
    nblk = n // MOE_ROWS
    grid_spec = pltpu.PrefetchScalarGridSpec(
        num_scalar_prefetch=2,
        grid=(nblk,),
        in_specs=[
            pl.BlockSpec((MOE_ROWS, d), lambda i, be, nu: (i, 0)),
            pl.BlockSpec((None, None, d, f), lambda i, be, nu: (layer, be[i], 0, 0)),
            pl.BlockSpec((None, None, d, f), lambda i, be, nu: (layer, be[i], 0, 0)),
            pl.BlockSpec((None, None, f, d), lambda i, be, nu: (layer, be[i], 0, 0)),
        ],
        out_specs=pl.BlockSpec((MOE_ROWS, d), lambda i, be, nu: (i, 0)),
        scratch_shapes=[pltpu.VMEM((d, f), BF16), pltpu.VMEM((d, f), BF16), pltpu.VMEM((f, d), BF16)],
    )
    return pl.pallas_call(
        _expert_kernel,
        grid_spec=grid_spec,
        out_shape=jax.ShapeDtypeStruct((n, d), F32),
        compiler_params=_cparams(("arbitrary",)),
        name="experts",
    )(blk_e, n_used, xs, wg, wu, wd)


def _combine_kernel(x1_ref, y0_ref, y1_ref, rt_ref, gt2_ref, fg_ref, o_ref, *, final):
    rt = rt_ref[...]
    x2 = x1_ref[...] + gt2_ref[0] * (rt[:, 2:3] * y0_ref[...] + rt[:, 3:4] * y1_ref[...])
    if final:
        x2 = x2 * lax.rsqrt(jnp.mean(x2 * x2, axis=-1, keepdims=True) + EPS) * fg_ref[...]
    o_ref[...] = x2


def combine_call(x1, ya, rt, gt2, fg, tm, seq_tiles, final):
    t, d = x1.shape
    mod_rows = gt2.shape[1]
    if mod_rows == 1:
        mod_map = lambda i: (i // seq_tiles, 0, 0)
    else:
        mod_map = lambda i: (0, 0, 0)
    nt = t // tm
    return pl.pallas_call(
        functools.partial(_combine_kernel, final=final),
        grid=(nt,),
        in_specs=[
            pl.BlockSpec((tm, d), lambda i: (i, 0)),
            pl.BlockSpec((tm, d), lambda i: (i, 0)),
            pl.BlockSpec((tm, d), lambda i: (i + nt, 0)),
            pl.BlockSpec((tm, LANES), lambda i: (i, 0)),
            pl.BlockSpec((1, mod_rows, d), mod_map),
            pl.BlockSpec((1, d), lambda i: (0, 0)),
        ],
        out_specs=pl.BlockSpec((tm, d), lambda i: (i, 0)),
        out_shape=jax.ShapeDtypeStruct((t, d), F32),
        compiler_params=_cparams(("arbitrary",)),
        name="combine",
    )(x1, ya, ya, rt, gt2, fg)


def moe_block(h2, x1, rt, cnt, gt2, fg, wg, wu, wd, layer, tm, seq_tiles, final):
    t, d = h2.shape
    a = t * EXPERT_TOPK
    eid = rt[:, :EXPERT_TOPK].astype(jnp.int32).T
    rank = rt[:, 4:4 + EXPERT_TOPK].astype(jnp.int32).T
    counts = cnt[0, :N_EXPERTS].astype(jnp.int32)
    padded = (counts + MOE_ROWS - 1) // MOE_ROWS * MOE_ROWS
    pad_ends = jnp.cumsum(padded)
    pad_starts = pad_ends - padded
    dest = (pad_starts[eid] + rank).astype(jnp.int32).reshape(-1)
    nblk = -(-a // MOE_ROWS) + N_EXPERTS
    n_used = (pad_ends[-1] // MOE_ROWS).astype(jnp.int32)
    blk_start = jnp.arange(nblk, dtype=jnp.int32) * MOE_ROWS
    blk_e = jnp.minimum(jnp.searchsorted(pad_ends, blk_start, side='right'), N_EXPERTS - 1).astype(jnp.int32)
    last_e = blk_e[jnp.maximum(n_used - 1, 0)]
    blk_e = jnp.where(jnp.arange(nblk) < n_used, blk_e, last_e)

    xs = scatter_call(dest, h2, jnp.zeros((nblk * MOE_ROWS, d), h2.dtype), tm)
    yb = expert_call(blk_e, n_used.reshape(1), xs, wg, wu, wd, layer)
    rows = min(MOE_ROWS, a)
    ya = gather_call(dest, yb, rows)
    return combine_call(x1, ya, rt, gt2, fg, tm, seq_tiles, final)


def layer_prompt(x, mods, lw, layer, rope, b, s, lam_init, final, fg):
    t, d = x.shape
    bw = d // 2
    sh1, sc1, gt1, sh2, sc2, gt2 = mods
    tm_in = min(512, s)
    (uv, gates, mk, mv, dk, dv, mqt, dqt, mkb, dkb, mvt, dvt) = inproj_attn_call(
        x, sc1, sh1, lw['n1'], lw['win'], rope, lw['lng'], lw['lnb'], tm_in, b, s)
    ob_t = attn_call(mqt, mkb, mvt, 1, True)
    oc_t = attn_call(dqt, dkb, dvt, 2, False, lw['dlam'], lw['subg_col'], lam_init)
    tm = 256
    x1, h2, rt, cnt = merge_call(x, uv, gates, ob_t.reshape(b, bw, s), oc_t.reshape(b, bw, s), gt1, sc2, sh2,
                            lw['n2'], lw['ws'], lw['bs_tab'], lw['wbr'], lw['wo'], lw['wr'], lw['br'],
                            tm, s // tm, 0, 0, True)
    x2 = moe_block(h2, x1, rt, cnt, gt2, fg, lw['weg'], lw['weu'], lw['wed'], layer, tm, s // tm, final)
    return x2, (mk, mv, dk, dv)


def layer_sample(x, mods, lw, layer, rope, db, ds, lam_init, final, fg, caches, page_table):
    t, d = x.shape
    bw = d // 2
    sh1, sc1, gt1, sh2, sc2, gt2 = mods
    zt = inproj_call(x, sc1, sh1, lw['n1'], lw['win'], rope, lw['lng'], lw['lnb'], t, 1)
    scale = HEAD_DIM ** -0.5
    eye = jnp.eye(N_UNITS, dtype=F32)

    def qbd(col0):
        q = zt[:, col0:col0 + bw].reshape(db, ds, N_UNITS, HEAD_DIM) * scale
        return jnp.einsum('btud,uw->btuwd', q, eye).reshape(db, ds * N_UNITS, bw)

    def pad_new(col0):
        a = zt[:, col0:col0 + bw].reshape(db, ds, bw)
        return jnp.pad(a, ((0, 0), (0, NEW_ROWS - ds), (0, 0)))

    unit = np.arange(ds * N_UNITS) % N_UNITS
    col = np.arange(bw)
    om_moba = jnp.asarray((col[None, :] // HEAD_DIM == unit[:, None]).astype(np.float32))
    om_diff = jnp.asarray((col[None, :] // (2 * HEAD_DIM) == unit[:, None] // 2).astype(np.float32))
    ckt, cvt, cdkt, cdv = caches
    ob = sattn_call(page_table, qbd(2 * bw), ckt, cvt, pad_new(3 * bw), pad_new(4 * bw), om_moba,
                    lw['dlam'], lw['subg_row'], layer, True, lam_init, ds).reshape(t, bw)
    oc = sattn_call(page_table, qbd(5 * bw), cdkt, cdv, pad_new(6 * bw), pad_new(7 * bw), om_diff,
                    lw['dlam'], lw['subg_row'], layer, False, lam_init, ds).reshape(t, bw)
    x1, h2, rt, cnt = merge_call(x, zt, zt, ob, oc, gt1, sc2, sh2, lw['n2'], lw['ws_s'], lw['bs_tab_s'], lw['wbr'],
                            lw['wo'], lw['wr'], lw['br'], t, 1, 0, (8 * bw) // d, False)
    x2 = moe_block(h2, x1, rt, cnt, gt2, fg, lw['weg'], lw['weu'], lw['wed'], layer, t, 1, final)
    outs = (zt[:, 3 * bw:4 * bw], zt[:, 4 * bw:5 * bw], zt[:, 6 * bw:7 * bw], zt[:, 7 * bw:8 * bw],
            zt[:, bw:2 * bw])
    return x2, outs


def kernel(x_prompt, x_sample, cache_moba_k, cache_moba_v, cache_diff_k, cache_diff_v, page_table, c_prompt, c_sample, norm1_g, norm2_g, final_g, w_ada, b_ada, w_in, gm_ln_g, gm_ln_b, gm_ws, gm_bs, diff_lambda, diff_subln_g, w_branch, w_out, w_group, b_group, w_router, b_router, w_gate_e, w_up_e, w_down_e):
    b, s, d = x_prompt.shape
    db, ds, _ = x_sample.shape
    depth = w_in.shape[0]
    bw = d // 2
    gd = bw // GM_GROUPS
    npool, page = cache_moba_k.shape[1], cache_moba_k.shape[2]
    past_len = page_table.shape[1] * page
    ts = db * ds
    assert s % ATT_BLOCK == 0 and ts % GM_CHUNK == 0 and GM_CHUNK % ds == 0

    rope_p = rope_tables(jnp.arange(s, dtype=jnp.int32))
    rope_s = rope_tables(jnp.tile(past_len + jnp.arange(ds, dtype=jnp.int32), db))
    caches = (
        jnp.transpose(cache_moba_k, (0, 1, 3, 4, 2)).reshape(depth, npool, bw, page),
        jnp.transpose(cache_moba_v, (0, 1, 3, 4, 2)).reshape(depth, npool, bw, page),
        jnp.transpose(cache_diff_k, (0, 1, 3, 4, 5, 2)).reshape(depth, npool, bw, page),
        cache_diff_v.reshape(depth, npool, page * cache_diff_v.shape[3], cache_diff_v.shape[4]),
    )

    nc = b + db
    ncp = -(-nc // 8) * 8
    c_all = jnp.pad(jnp.concatenate([c_prompt, c_sample], axis=0), ((0, ncp - nc), (0, 0)))
    tril = jnp.tril(jnp.ones((GM_CHUNK, GM_CHUNK), F32))
    fg = final_g.reshape(1, d)

    xp = x_prompt.reshape(b * s, d)
    xs = x_sample.reshape(ts, d)
    outs_p, outs_s = [], []
    for l in range(depth):
        lam_init = 0.8 - 0.6 * math.exp(-0.3 * l)
        final = l == depth - 1
        m = ada_call(c_all, w_ada, b_ada, l)
        parts = [m[:, i * d:(i + 1) * d] for i in range(6)]
        mods_p = tuple(p[:b].reshape(b, 1, d) for p in parts)
        mods_s = tuple(jnp.repeat(p[b:nc], ds, axis=0).reshape(1, ts, d) for p in parts)

        ws_tril = gm_ws[l] * tril[None]
        ws_small = ws_tril[:, :ds, :ds]
        ws_s = jnp.einsum('ab,gts->gatbs', jnp.eye(GM_CHUNK // ds, dtype=F32), ws_small)
        ws_s = ws_s.reshape(GM_GROUPS, GM_CHUNK, GM_CHUNK)
        bs_tab = jnp.repeat(gm_bs[l].T, gd, axis=1)
        bs_tab_s = jnp.tile(bs_tab[:ds], (GM_CHUNK // ds, 1))
        wr = jnp.pad(jnp.concatenate([w_group[l], w_router[l]], axis=1),
                     ((0, 0), (0, LANES - N_GROUPS - N_EXPERTS)))
        wr_hi = wr.astype(BF16)
        wr = jnp.stack([wr_hi, (wr - wr_hi.astype(F32)).astype(BF16)])
        br = jnp.pad(jnp.concatenate([b_group[l], b_router[l]]), (0, LANES - N_GROUPS - N_EXPERTS))
        lw = dict(
            n1=norm1_g[l].reshape(1, d), n2=norm2_g[l].reshape(1, d),
            win=w_in[l].astype(BF16), lng=gm_ln_g[l].reshape(1, bw), lnb=gm_ln_b[l].reshape(1, bw),
            ws=ws_tril.astype(BF16), bs_tab=bs_tab, ws_s=ws_s.astype(BF16), bs_tab_s=bs_tab_s,
            dlam=diff_lambda[l], subg_col=diff_subln_g[l].reshape(-1, 1), subg_row=diff_subln_g[l].reshape(1, -1),
            wbr=w_branch[l].astype(BF16), wo=w_out[l].astype(BF16), wr=wr, br=br.reshape(1, LANES),
            weg=w_gate_e, weu=w_up_e, wed=w_down_e,
        )
        xp, op = layer_prompt(xp, mods_p, lw, l, rope_p, b, s, lam_init, final, fg)
        xs, os_ = layer_sample(xs, mods_s, lw, l, rope_s, db, ds, lam_init, final, fg, caches, page_table)
        outs_p.append(op)
        outs_s.append(os_)

    nh = N_UNITS
    stack = lambda lst, i, shape: jnp.stack([o[i].reshape(shape) for o in lst])
    return (
        xp.reshape(b, s, d), xs.reshape(db, ds, d),
        stack(outs_p, 0, (b, s, nh, HEAD_DIM)), stack(outs_p, 1, (b, s, nh, HEAD_DIM)),
        stack(outs_p, 2, (b, s, nh // 2, 2, HEAD_DIM)), stack(outs_p, 3, (b, s, nh // 2, 2 * HEAD_DIM)),
        stack(outs_s, 0, (db, ds, nh, HEAD_DIM)), stack(outs_s, 1, (db, ds, nh, HEAD_DIM)),
        stack(outs_s, 2, (db, ds, nh // 2, 2, HEAD_DIM)), stack(outs_s, 3, (db, ds, nh // 2, 2 * HEAD_DIM)),
        stack(outs_s, 4, (db, ds, bw)),
    )
```

```python
import functools
import math

import jax
import jax.numpy as jnp
import numpy as np
from jax import lax
from jax.experimental import pallas as pl
from jax.experimental.pallas import tpu as pltpu

F32 = jnp.float32
BF16 = jnp.bfloat16
HIGHEST = lax.Precision.HIGHEST

EPS = 1e-6
NEG = -1e30
BIG = 1e30

N_BRANCHES = 3
GM_GROUPS = 4
GM_CHUNK = 128
N_UNITS = 8
HEAD_DIM = 64
ROT = HEAD_DIM // 4
ROPE_THETA = 500000.0
ATT_BLOCK = 256
ATT_CHAINS = 8
MOBA_TOPK = 3
SAMPLE_BLOCKS_PER_STEP = 4
NEW_ROWS = 16
N_GROUPS = 4
EXPERTS_PER_GROUP = 8
N_EXPERTS = N_GROUPS * EXPERTS_PER_GROUP
EXPERT_TOPK = 2
MOE_ROWS = 256
GATHER_UNROLL = 8
LANES = 128
VMEM_LIMIT = 48 * 1024 * 1024
NT_DIMS = (((1,), (1,)), ((), ()))


def _cparams(sem):
    return pltpu.CompilerParams(dimension_semantics=sem, vmem_limit_bytes=VMEM_LIMIT)


def _ada_kernel(c_ref, w_ref, b_ref, o_ref):
    c = c_ref[...]
    a = c * jax.nn.sigmoid(c)
    o_ref[...] = jnp.dot(a, w_ref[...], preferred_element_type=F32, precision=HIGHEST) + b_ref[...]


def ada_call(c_all, w_ada, b_ada, layer):
    m, d = c_all.shape
    n = w_ada.shape[-1]
    tn = n // 4
    return pl.pallas_call(
        _ada_kernel,
        grid=(n // tn,),
        in_specs=[
            pl.BlockSpec((m, d), lambda j: (0, 0)),
            pl.BlockSpec((None, d, tn), lambda j: (layer, 0, j)),
            pl.BlockSpec((None, 1, tn), lambda j: (layer, 0, j)),
        ],
        out_specs=pl.BlockSpec((m, tn), lambda j: (0, j)),
        out_shape=jax.ShapeDtypeStruct((m, n), F32),
        compiler_params=_cparams(("arbitrary",)),
        name="ada",
    )(c_all, w_ada, b_ada.reshape(b_ada.shape[0], 1, n))


def _norm_mod(x_ref, sc_ref, sh_ref, g_ref):
    x = x_ref[...]
    y = x * lax.rsqrt(jnp.mean(x * x, axis=-1, keepdims=True) + EPS) * g_ref[...]
    return (y * (1.0 + sc_ref[0]) + sh_ref[0]).astype(BF16)


def _gelu_ln(z, lng_ref, lnb_ref):
    a = jax.nn.gelu(z)
    mu = jnp.mean(a, axis=-1, keepdims=True)
    ac = a - mu
    var = jnp.mean(ac * ac, axis=-1, keepdims=True)
    return ac * lax.rsqrt(var + EPS) * lng_ref[...] + lnb_ref[...]


def _rope(z, rc_ref, rs1_ref, rs2_ref):
    width = z.shape[-1]
    reps = width // LANES
    c = jnp.concatenate([rc_ref[...]] * reps, axis=1)
    s1 = jnp.concatenate([rs1_ref[...]] * reps, axis=1)
    s2 = jnp.concatenate([rs2_ref[...]] * reps, axis=1)
    up = pltpu.roll(z, width - ROT // 2, 1)
    dn = pltpu.roll(z, ROT // 2, 1)
    return z * c + up * s1 + dn * s2


def _inproj_kernel(x_ref, sc_ref, sh_ref, g_ref, w_ref, rc_ref, rs1_ref, rs2_ref, lng_ref, lnb_ref,
                   o_ref, h_scr):
    j = pl.program_id(1)

    @pl.when(j == 0)
    def _():
        h_scr[...] = _norm_mod(x_ref, sc_ref, sh_ref, g_ref)

    def proj():
        return jnp.dot(h_scr[...], w_ref[...], preferred_element_type=F32)

    @pl.when(j == 0)
    def _():
        o_ref[...] = jax.nn.gelu(proj())

    @pl.when(j == 1)
    def _():
        o_ref[...] = _gelu_ln(proj(), lng_ref, lnb_ref)

    @pl.when((j == 2) | (j == 3) | (j == 5) | (j == 6))
    def _():
        o_ref[...] = _rope(proj(), rc_ref, rs1_ref, rs2_ref)

    @pl.when((j == 4) | (j == 7))
    def _():
        o_ref[...] = proj()

    @pl.when(j >= 8)
    def _():
        o_ref[...] = jax.nn.sigmoid(proj())


def _inproj_attn_kernel(x_ref, sc_ref, sh_ref, g_ref, w_ref, rc_ref, rs1_ref, rs2_ref, lng_ref, lnb_ref,
                        uv_ref, gates_ref, mk_ref, mv_ref, dk_ref, dv_ref,
                        mqt_ref, dqt_ref, mkb_ref, dkb_ref, mvt_ref, dvt_ref, h_scr):
    j = pl.program_id(1)
    scale = HEAD_DIM ** -0.5 * math.log2(math.e)

    @pl.when(j == 0)
    def _():
        h_scr[...] = _norm_mod(x_ref, sc_ref, sh_ref, g_ref)

    def proj():
        return jnp.dot(h_scr[...], w_ref[...], preferred_element_type=F32)

    def roped():
        return _rope(proj(), rc_ref, rs1_ref, rs2_ref)

    def put_qt(ref, zr):
        zt = (zr * scale).T.astype(BF16)
        for u in range(ref.shape[0]):
            ref[u] = zt[u * HEAD_DIM:(u + 1) * HEAD_DIM, :]

    def put_kb(ref, zr):
        zb = zr.astype(BF16)
        for u in range(ref.shape[0]):
            for kb in range(ref.shape[1]):
                ref[u, kb] = zb[kb * ATT_BLOCK:(kb + 1) * ATT_BLOCK, u * HEAD_DIM:(u + 1) * HEAD_DIM]

    def put_vt(ref, zt):
        zb = zt.astype(BF16)
        dv = ref.shape[2]
        for n in range(ref.shape[0]):
            for kb in range(ref.shape[1]):
                ref[n, kb] = zb[n * dv:(n + 1) * dv, kb * ATT_BLOCK:(kb + 1) * ATT_BLOCK]

    @pl.when(j == 0)
    def _():
        uv_ref[...] = jax.nn.gelu(proj())

    @pl.when(j == 1)
    def _():
        uv_ref[...] = _gelu_ln(proj(), lng_ref, lnb_ref)

    @pl.when(j == 2)
    def _():
        put_qt(mqt_ref, roped())

    @pl.when(j == 3)
    def _():
        zr = roped()
        mk_ref[...] = zr.T
        put_kb(mkb_ref, zr)

    @pl.when(j == 4)
    def _():
        zt = proj().T
        mv_ref[...] = zt
        put_vt(mvt_ref, zt)

    @pl.when(j == 5)
    def _():
        put_qt(dqt_ref, roped())

    @pl.when(j == 6)
    def _():
        zr = roped()
        dk_ref[...] = zr.T
        put_kb(dkb_ref, zr)

    @pl.when(j == 7)
    def _():
        z = proj()
        nh = dv_ref.shape[0] // z.shape[0]
        for hh in range(nh):
            dv_ref[pl.ds(hh, z.shape[0], stride=nh), :] = z[:, hh * (2 * HEAD_DIM):(hh + 1) * (2 * HEAD_DIM)]
        put_vt(dvt_ref, z.T)

    @pl.when(j >= 8)
    def _():
        gates_ref[...] = jax.nn.sigmoid(proj())


def _inproj_in_specs(tm, d, tn, mod_rows, seq_tiles):
    if mod_rows == 1:
        mod_map = lambda i, j: (i // seq_tiles, 0, 0)
    else:
        mod_map = lambda i, j: (0, 0, 0)
    rope_map = lambda i, j: (i % seq_tiles, 0)
    return [
        pl.BlockSpec((tm, d), lambda i, j: (i, 0)),
        pl.BlockSpec((1, mod_rows, d), mod_map),
        pl.BlockSpec((1, mod_rows, d), mod_map),
        pl.BlockSpec((1, d), lambda i, j: (0, 0)),
        pl.BlockSpec((d, tn), lambda i, j: (0, j)),
        pl.BlockSpec((tm, LANES), rope_map),
        pl.BlockSpec((tm, LANES), rope_map),
        pl.BlockSpec((tm, LANES), rope_map),
        pl.BlockSpec((1, tn), lambda i, j: (0, 0)),
        pl.BlockSpec((1, tn), lambda i, j: (0, 0)),
    ]


def inproj_call(x, sc, sh, g, w_bf, rope, lng, lnb, tm, seq_tiles):
    t, d = x.shape
    n = w_bf.shape[1]
    tn = 512
    rc, rs1, rs2 = rope
    return pl.pallas_call(
        _inproj_kernel,
        grid=(t // tm, n // tn),
        in_specs=_inproj_in_specs(tm, d, tn, sc.shape[1], seq_tiles),
        out_specs=pl.BlockSpec((tm, tn), lambda i, j: (i, j)),
        out_shape=jax.ShapeDtypeStruct((t, n), F32),
        scratch_shapes=[pltpu.VMEM((tm, d), BF16)],
        compiler_params=_cparams(("arbitrary", "arbitrary")),
        name="inproj",
    )(x, sc, sh, g, w_bf, rc, rs1, rs2, lng, lnb)


def inproj_attn_call(x, sc, sh, g, w_bf, rope, lng, lnb, tm, b, s):
    t, d = x.shape
    tn = 512
    n = w_bf.shape[1]
    seq_tiles = s // tm
    nb = s // ATT_BLOCK
    kbt = tm // ATT_BLOCK
    nu, dh = N_UNITS, HEAD_DIM
    rc, rs1, rs2 = rope
    bi = lambda i: i // seq_tiles
    si = lambda i: i % seq_tiles
    nvh = nu // 2
    tr_spec = pl.BlockSpec((None, tn, tm), lambda i, j: (bi(i), 0, si(i)))
    qt_spec =pl.BlockSpec((None, nu, dh, tm), lambda i, j: (bi(i), 0, 0, si(i)))
    kb_spec = pl.BlockSpec((None, nu, kbt, ATT_BLOCK, dh), lambda i, j: (bi(i), 0, si(i), 0, 0))

    def vt_spec(nv):
        return pl.BlockSpec((None, nv, kbt, tn // nv, ATT_BLOCK), lambda i, j: (bi(i), 0, si(i), 0, 0))

    sds = jax.ShapeDtypeStruct
    return pl.pallas_call(
        _inproj_attn_kernel,
        grid=(t // tm, n // tn),
        in_specs=_inproj_in_specs(tm, d, tn, sc.shape[1], seq_tiles),
        out_specs=[
            pl.BlockSpec((tm, tn), lambda i, j: (i, jnp.minimum(j, 1))),
            pl.BlockSpec((tm, tn), lambda i, j: (i, jnp.maximum(j - 8, 0))),
            tr_spec, tr_spec, tr_spec, pl.BlockSpec((tm * nvh, tn // nvh), lambda i, j: (i, 0)),
            qt_spec, qt_spec, kb_spec, kb_spec, vt_spec(nu), vt_spec(nu // 2),
        ],
        out_shape=[
            sds((t, 2 * tn), F32), sds((t, n - 8 * tn), F32),
            sds((b, tn, s), F32), sds((b, tn, s), F32), sds((b, tn, s), F32),
            sds((t * nvh, tn // nvh), F32),
            sds((b, nu, dh, s), BF16), sds((b, nu, dh, s), BF16),
            sds((b, nu, nb, ATT_BLOCK, dh), BF16), sds((b, nu, nb, ATT_BLOCK, dh), BF16),
            sds((b, nu, nb, dh, ATT_BLOCK), BF16), sds((b, nu // 2, nb, 2 * dh, ATT_BLOCK), BF16),
        ],
        scratch_shapes=[pltpu.VMEM((tm, d), BF16)],
        compiler_params=_cparams(("arbitrary", "arbitrary")),
        name="inproj_attn",
    )(x, sc, sh, g, w_bf, rc, rs1, rs2, lng, lnb)


def rope_tables(pos):
    half = ROT // 2
    inv = ROPE_THETA ** (-(jnp.arange(half, dtype=F32) * 2.0 / ROT))
    ang = pos.astype(F32)[:, None] * inv[None, :]
    cos, sin = jnp.cos(ang), jnp.sin(ang)
    ones = jnp.ones((pos.shape[0], HEAD_DIM - ROT), F32)
    zeros8 = jnp.zeros((pos.shape[0], half), F32)
    zrest = jnp.zeros_like(ones)
    c = jnp.concatenate([cos, cos, ones], axis=1)
    s1 = jnp.concatenate([-sin, zeros8, zrest], axis=1)
    s2 = jnp.concatenate([zeros8, sin, zrest], axis=1)
    tile = lambda a: jnp.concatenate([a] * (LANES // HEAD_DIM), axis=1)
    return tile(c), tile(s1), tile(s2)


def _diff_lambda(dl_ref, lam_init):
    dl = dl_ref[...]
    return (jnp.exp(jnp.sum(dl[0:1] * dl[1:2], axis=1, keepdims=True))
            - jnp.exp(jnp.sum(dl[2:3] * dl[3:4], axis=1, keepdims=True)) + lam_init)


def _attn_kernel(*refs, nmaps, use_sel, lam_init):
    if use_sel:
        q_ref, k_ref, v_ref, o_ref, kmean_scr, sel_scr = refs
    else:
        q_ref, k_ref, v_ref, dl_ref, sg_ref, o_ref = refs
    qi = pl.program_id(2)
    nchain = q_ref.shape[0]
    nb = k_ref.shape[1]
    tq = q_ref.shape[-1]
    blk = k_ref.shape[2]

    if use_sel:
        @pl.when(qi == 0)
        def _():
            for c in range(nchain):
                kmean_scr[c] = jnp.mean(k_ref[c].astype(F32), axis=1)

        row = lax.broadcasted_iota(jnp.int32, (nb, tq), 0)
        for c in range(nchain):
            gate = jnp.dot(kmean_scr[c], q_ref[c].astype(F32), preferred_element_type=F32,
                           precision=HIGHEST)
            past = row < qi
            gate = jnp.where(past, gate, NEG)
            sel = jnp.zeros((nb, tq), F32)
            for _ in range(MOBA_TOPK):
                top = jnp.max(gate, axis=0, keepdims=True)
                first = jnp.min(jnp.where(gate == top, row, nb), axis=0, keepdims=True)
                hit = row == first
                sel = jnp.where(hit & past, 1.0, sel)
                gate = jnp.where(hit, -jnp.inf, gate)
            sel_scr[c] = sel

    kr = lax.broadcasted_iota(jnp.int32, (blk, tq), 0)
    qc = lax.broadcasted_iota(jnp.int32, (blk, tq), 1)
    qts = [q_ref[c] for c in range(nchain)]
    ss = [jnp.dot(k_ref[c, qi], qts[c], preferred_element_type=F32) for c in range(nchain)]
    stats = []
    for c in range(nchain):
        s = jnp.where(kr <= qc, ss[c], NEG)
        m0 = jnp.max(s, axis=0, keepdims=True)
        p = jnp.exp2(s - m0)
        stats.append((m0, jnp.sum(p, axis=0, keepdims=True), p.astype(BF16)))
    carry0 = tuple(
        (stats[c][0], stats[c][1],
         jnp.dot(v_ref[c // nmaps, qi], stats[c][2], preferred_element_type=F32))
        for c in range(nchain))

    def kv_body(kj, carry):
        ss = [jnp.dot(k_ref[c, kj], qts[c], preferred_element_type=F32) for c in range(nchain)]
        upd = []
        for c in range(nchain):
            m, l, _ = carry[c]
            smax = jnp.max(ss[c], axis=0, keepdims=True)
            if use_sel:
                on = sel_scr[c, pl.ds(kj, 1), :] > 0.5
                m_new = jnp.where(on, jnp.maximum(m, smax), m)
                m_use = jnp.where(on, m_new, BIG)
            else:
                m_new = jnp.maximum(m, smax)
                m_use = m_new
            p = jnp.exp2(ss[c] - m_use)
            alpha = jnp.exp2(m - m_new)
            upd.append((m_new, alpha * l + jnp.sum(p, axis=0, keepdims=True), alpha, p.astype(BF16)))
        return tuple(
            (upd[c][0], upd[c][1],
             upd[c][2] * carry[c][2] + jnp.dot(v_ref[c // nmaps, kj], upd[c][3], preferred_element_type=F32))
            for c in range(nchain))

    carry = lax.fori_loop(0, qi, kv_body, tuple(carry0))
    outs = [acc / l for (_, l, acc) in carry]

    if use_sel:
        for c in range(nchain):
            o_ref[c] = outs[c]
    else:
        lam = _diff_lambda(dl_ref, lam_init)
        for hh in range(nchain // 2):
            o = outs[2 * hh] - lam * outs[2 * hh + 1]
            ms = jnp.mean(o * o, axis=0, keepdims=True)
            o_ref[hh] = o * lax.rsqrt(ms + EPS) * sg_ref[...] * (1.0 - lam_init)


def attn_call(qt, k, vt, nmaps, use_sel, dlam=None, subg=None, lam_init=0.0):
    b, nu, dh, s = qt.shape
    nb, blk = k.shape[2], k.shape[3]
    nv, dv = vt.shape[1], vt.shape[3]
    tq = blk
    nc = ATT_CHAINS
    nvb = nc // nmaps
    kern = functools.partial(_attn_kernel, nmaps=nmaps, use_sel=use_sel, lam_init=lam_init)
    in_specs = [
        pl.BlockSpec((None, nc, dh, tq), lambda bi, n, qi: (bi, n, 0, qi)),
        pl.BlockSpec((None, nc, nb, blk, dh), lambda bi, n, qi: (bi, n, 0, 0, 0)),
        pl.BlockSpec((None, nvb, nb, dv, blk), lambda bi, n, qi: (bi, n, 0, 0, 0)),
    ]
    args = [qt, k, vt]
    scratch = []
    if use_sel:
        scratch = [pltpu.VMEM((nc, nb, dh), F32), pltpu.VMEM((nc, nb, tq), F32)]
    else:
        in_specs += [pl.BlockSpec(dlam.shape, lambda bi, n, qi: (0, 0)),
                     pl.BlockSpec(subg.shape, lambda bi, n, qi: (0, 0))]
        args += [dlam, subg]
    return pl.pallas_call(
        kern,
        grid=(b, nu // nc, s // tq),
        in_specs=in_specs,
        out_specs=pl.BlockSpec((None, nvb, dv, tq), lambda bi, n, qi: (bi, n, 0, qi)),
        out_shape=jax.ShapeDtypeStruct((b, nv, dv, s), F32),
        scratch_shapes=scratch,
        compiler_params=_cparams(("arbitrary", "arbitrary", "arbitrary")),
        name="moba_attn" if use_sel else "diff_attn",
    )(*args)


def _sattn_kernel(pt_ref, *refs, use_sel, nblk, bps, n_new, lam_init):
    del pt_ref
    npg = 2 * bps
    qbd_ref = refs[0]
    kt_refs = refs[1:1 + npg]
    v_refs = refs[1 + npg:1 + 2 * npg]
    kn_ref, vn_ref, om_ref, dl_ref, sg_ref, o_ref, opart, g_scr, m_scr, l_scr = refs[1 + 2 * npg:]
    j = pl.program_id(1)
    qf = qbd_ref[0]
    rows, wdt = qf.shape
    q_hi = qf.astype(BF16)
    q_lo = (qf - q_hi.astype(F32)).astype(BF16)
    lhs = jnp.concatenate([q_hi, q_lo], axis=0)
    lane = lax.broadcasted_iota(jnp.int32, (rows, LANES), 1)
    om = om_ref[...]

    @pl.when(j == 0)
    def _():
        g_scr[...] = jnp.full((rows, LANES), NEG, F32)
        m_scr[...] = jnp.full((rows, LANES), NEG, F32)
        l_scr[...] = jnp.zeros((rows, LANES), F32)

    def scores(kt):
        s2 = jnp.dot(lhs, kt, preferred_element_type=F32)
        return s2[:rows] + s2[rows:]

    gacc, macc, lacc = g_scr[...], m_scr[...], l_scr[...]
    ss = [jnp.concatenate([scores(kt_refs[2 * bb + w][...].astype(BF16)) for w in range(2)], axis=1)
          for bb in range(bps)]
    stats = []
    for bb in range(bps):
        mj = jnp.max(ss[bb], axis=-1, keepdims=True)
        p = jnp.exp(ss[bb] - mj)
        stats.append((mj, jnp.sum(p, axis=-1, keepdims=True), p.astype(BF16)))
    for bb in range(bps):
        blk_id = j * bps + bb
        s = ss[bb]
        mj, lj, pb = stats[bb]
        oj = None
        for w in range(2):
            pw = pb[:, w * LANES:(w + 1) * LANES]
            vref = v_refs[2 * bb + w]
            if use_sel:
                ow = lax.dot_general(pw, vref[...].astype(BF16), NT_DIMS, preferred_element_type=F32)
            else:
                nh = vref.shape[0] // LANES
                ow = jnp.concatenate(
                    [jnp.dot(pw, vref[pl.ds(hh, LANES, stride=nh), :].astype(BF16),
                             preferred_element_type=F32) for hh in range(nh)], axis=1)
            oj = ow if oj is None else oj + ow
        opart[blk_id] = oj * om
        hit = lane == blk_id
        if use_sel:
            gacc = jnp.where(hit, jnp.sum(s, axis=-1, keepdims=True), gacc)
        macc = jnp.where(hit, mj, macc)
        lacc = jnp.where(hit, lj, lacc)
    g_scr[...] = gacc
    m_scr[...] = macc
    l_scr[...] = lacc

    @pl.when(j == nblk // bps - 1)
    def _():
        zpad = jnp.zeros((LANES - NEW_ROWS, wdt), BF16)
        kn = jnp.concatenate([kn_ref[0].astype(BF16), zpad], axis=0)
        vn = jnp.concatenate([vn_ref[0].astype(BF16), zpad], axis=0)
        so2 = lax.dot_general(lhs, kn, NT_DIMS, preferred_element_type=F32)
        so = so2[:rows] + so2[rows:]
        rtok = lax.broadcasted_iota(jnp.int32, (rows, LANES), 0) // N_UNITS
        so = jnp.where((lane <= rtok) & (lane < n_new), so, NEG)
        mo = jnp.max(so, axis=-1, keepdims=True)
        po = jnp.exp(so - mo)
        lo = jnp.sum(po, axis=-1, keepdims=True)
        oo = jnp.dot(po.astype(BF16), vn, preferred_element_type=F32) * om

        inb = lane < nblk
        if use_sel:
            g = jnp.where(inb, gacc, -jnp.inf)
            selm = jnp.zeros((rows, LANES), F32)
            for _ in range(min(MOBA_TOPK, nblk)):
                top = jnp.max(g, axis=-1, keepdims=True)
                first = jnp.min(jnp.where(g == top, lane, LANES), axis=-1, keepdims=True)
                hit = lane == first
                selm = jnp.where(hit, 1.0, selm)
                g = jnp.where(hit, -jnp.inf, g)
            selb = selm > 0.5
        else:
            selb = inb
        mt = jnp.maximum(jnp.max(jnp.where(selb, macc, NEG), axis=-1, keepdims=True), mo)
        w = jnp.where(selb, jnp.exp(macc - mt), 0.0)
        wo = jnp.exp(mo - mt)
        ltot = jnp.sum(w * lacc, axis=-1, keepdims=True) + wo * lo
        acc = wo * oo
        for jb in range(nblk):
            acc = acc + w[:, jb:jb + 1] * opart[jb]
        res = acc / ltot
        if not use_sel:
            lam = _diff_lambda(dl_ref, lam_init)
            rr = lax.broadcasted_iota(jnp.int32, (rows, 1), 0)
            res = res * jnp.where(rr % 2 == 0, 1.0, -lam)
        out = jnp.sum(res.reshape(rows // N_UNITS, N_UNITS, wdt), axis=1)
        if not use_sel:
            dv = sg_ref.shape[-1]
            segs = []
            for hh in range(wdt // dv):
                seg = out[:, hh * dv:(hh + 1) * dv]
                ms = jnp.mean(seg * seg, axis=-1, keepdims=True)
                segs.append(seg * lax.rsqrt(ms + EPS) * sg_ref[...] * (1.0 - lam_init))
            out = jnp.concatenate(segs, axis=1)
        o_ref[0] = out


def sattn_call(page_table, qbd, cache_kt, cache_v, knew, vnew, omask, dlam, subg, layer, use_sel,
               lam_init, n_new):
    db, rows, w = qbd.shape
    page = cache_kt.shape[3]
    n_pages = page_table.shape[1]
    nblk = n_pages * page // ATT_BLOCK
    bps = SAMPLE_BLOCKS_PER_STEP
    assert ATT_BLOCK == 2 * page and page == LANES and nblk <= LANES and nblk % bps == 0
    pt_flat = page_table.reshape(-1)

    def page_spec(arr, which):
        return pl.BlockSpec((None, None) + arr.shape[2:],
                            lambda b, j, pt: (layer, pt[b * n_pages + 2 * bps * j + which], 0, 0))

    kern = functools.partial(_sattn_kernel, use_sel=use_sel, nblk=nblk, bps=bps, n_new=n_new,
                             lam_init=lam_init)
    n_tok = rows // N_UNITS
    grid_spec = pltpu.PrefetchScalarGridSpec(
        num_scalar_prefetch=1,
        grid=(db, nblk // bps),
        in_specs=(
            [pl.BlockSpec((1, rows, w), lambda b, j, pt: (b, 0, 0))]
            + [page_spec(cache_kt, i) for i in range(2 * bps)]
            + [page_spec(cache_v, i) for i in range(2 * bps)]
            + [pl.BlockSpec((1, NEW_ROWS, w), lambda b, j, pt: (b, 0, 0)),
               pl.BlockSpec((1, NEW_ROWS, w), lambda b, j, pt: (b, 0, 0)),
               pl.BlockSpec((rows, w), lambda b, j, pt: (0, 0)),
               pl.BlockSpec(dlam.shape, lambda b, j, pt: (0, 0)),
               pl.BlockSpec(subg.shape, lambda b, j, pt: (0, 0))]
        ),
        out_specs=pl.BlockSpec((1, n_tok, w), lambda b, j, pt: (b, 0, 0)),
        scratch_shapes=[pltpu.VMEM((nblk, rows, w), F32), pltpu.VMEM((rows, LANES), F32),
                        pltpu.VMEM((rows, LANES), F32), pltpu.VMEM((rows, LANES), F32)],
    )
    return pl.pallas_call(
        kern,
        grid_spec=grid_spec,
        out_shape=jax.ShapeDtypeStruct((db, n_tok, w), F32),
        compiler_params=_cparams(("arbitrary", "arbitrary")),
        name="moba_sample" if use_sel else "diff_sample",
    )(pt_flat, qbd, *([cache_kt] * (2 * bps)), *([cache_v] * (2 * bps)), knew, vnew, omask, dlam, subg)


def _merge_kernel(x_ref, u_ref, v_ref, ob_ref, oc_ref, ga_ref, gb_ref, gc_ref, gt1_ref, sc2_ref,
                  sh2_ref, n2_ref, ws_ref, bs_ref, wbr_ref, wo_ref, wr_ref, br_ref,
                  x1_ref, h2_ref, rt_ref, cnt_ref, *, o_transposed):
    tm = x_ref.shape[0]
    gd = ws_ref.shape[-1]
    v = v_ref[...].astype(BF16)
    chunks = []
    for c in range(tm // GM_CHUNK):
        r0 = c * GM_CHUNK
        cols = [jnp.dot(ws_ref[g], v[r0:r0 + GM_CHUNK, g * gd:(g + 1) * gd], preferred_element_type=F32)
                for g in range(GM_GROUPS)]
        chunks.append(jnp.concatenate(cols, axis=1) + bs_ref[...])
    mixed = jnp.concatenate(chunks, axis=0) if len(chunks) > 1 else chunks[0]
    o_a = u_ref[...] * mixed
    if o_transposed:
        o_b = ob_ref[...].T
        o_c = oc_ref[...].T
    else:
        o_b = ob_ref[...]
        o_c = oc_ref[...]
    merged = (ga_ref[...] * jnp.dot(o_a.astype(BF16), wbr_ref[0], preferred_element_type=F32)
              + gb_ref[...] * jnp.dot(o_b.astype(BF16), wbr_ref[1], preferred_element_type=F32)
              + gc_ref[...] * jnp.dot(o_c.astype(BF16), wbr_ref[2], preferred_element_type=F32))
    y = jnp.dot(merged.astype(BF16), wo_ref[...], preferred_element_type=F32)
    x1 = x_ref[...] + gt1_ref[0] * y
    x1_ref[...] = x1
    h2 = x1 * lax.rsqrt(jnp.mean(x1 * x1, axis=-1, keepdims=True) + EPS) * n2_ref[...]
    h2 = h2 * (1.0 + sc2_ref[0]) + sh2_ref[0]
    h2_ref[...] = h2

    h_hi = h2.astype(BF16)
    h_lo = (h2 - h_hi.astype(F32)).astype(BF16)
    logits = (jnp.dot(h_hi, wr_ref[0], preferred_element_type=F32)
              + jnp.dot(h_lo, wr_ref[0], preferred_element_type=F32)
              + jnp.dot(h_hi, wr_ref[1], preferred_element_type=F32)) + br_ref[...]
    lane = lax.broadcasted_iota(jnp.int32, logits.shape, 1)
    gl = jnp.where(lane < N_GROUPS, logits, NEG)
    gmax = jnp.max(gl, axis=-1, keepdims=True)
    gsel = jnp.min(jnp.where(gl == gmax, lane, LANES), axis=-1, keepdims=True)
    gw = 1.0 / jnp.sum(jnp.exp(gl - gmax), axis=-1, keepdims=True)
    lo = N_GROUPS + EXPERTS_PER_GROUP * gsel
    el = jnp.where((lane >= lo) & (lane < lo + EXPERTS_PER_GROUP), logits, NEG)
    v1 = jnp.max(el, axis=-1, keepdims=True)
    i1 = jnp.min(jnp.where(el == v1, lane, LANES), axis=-1, keepdims=True)
    el2 = jnp.where(lane == i1, NEG, el)
    v2 = jnp.max(el2, axis=-1, keepdims=True)
    i2 = jnp.min(jnp.where(el2 == v2, lane, LANES), axis=-1, keepdims=True)
    e2 = jnp.exp(v2 - v1)
    w1 = gw / (1.0 + e2)
    w2 = gw * e2 / (1.0 + e2)

    @pl.when(pl.program_id(0) == 0)
    def _():
        cnt_ref[...] = jnp.zeros(cnt_ref.shape, F32)

    e1 = i1 - N_GROUPS
    e2i = i2 - N_GROUPS
    onehot = jnp.where((lane == e1) | (lane == e2i), 1.0, 0.0)
    rr = lax.broadcasted_iota(jnp.int32, (tm, tm), 0)
    cc = lax.broadcasted_iota(jnp.int32, (tm, tm), 1)
    lower = jnp.where(rr > cc, 1.0, 0.0).astype(BF16)
    before = jnp.dot(lower, onehot.astype(BF16), preferred_element_type=F32) + cnt_ref[0:1, :]
    r1 = jnp.sum(jnp.where(lane == e1, before, 0.0), axis=-1, keepdims=True)
    r2 = jnp.sum(jnp.where(lane == e2i, before, 0.0), axis=-1, keepdims=True)
    cnt_ref[...] = cnt_ref[...] + jnp.sum(onehot, axis=0, keepdims=True)

    vals = (e1.astype(F32), e2i.astype(F32), w1, w2, r1, r2)
    rt = jnp.zeros(logits.shape, F32)
    for k, val in enumerate(vals):
        rt = jnp.where(lane == k, val, rt)
    rt_ref[...] = rt


def merge_call(x, uv, gates, ob, oc, gt1, sc2, sh2, n2, ws_bf, bs_tab, wbr_bf, wo_bf, wr, br, tm,
               seq_tiles, uv_col0, gate_col0, o_transposed):
    t, d = x.shape
    bw = d // 2
    mod_rows = gt1.shape[1]
    if mod_rows == 1:
        mod_map = lambda i: (i // seq_tiles, 0, 0)
    else:
        mod_map = lambda i: (0, 0, 0)
    if o_transposed:
        o_spec = pl.BlockSpec((None, bw, tm), lambda i: (i // seq_tiles, 0, i % seq_tiles))
    else:
        o_spec = pl.BlockSpec((tm, bw), lambda i: (i, 0))
    full = lambda a: pl.BlockSpec(a.shape, lambda i: (0,) * a.ndim)
    return pl.pallas_call(
        functools.partial(_merge_kernel, o_transposed=o_transposed),
        grid=(t // tm,),
        in_specs=[
            pl.BlockSpec((tm, d), lambda i: (i, 0)),
            pl.BlockSpec((tm, bw), lambda i: (i, uv_col0)),
            pl.BlockSpec((tm, bw), lambda i: (i, uv_col0 + 1)),
            o_spec, o_spec,
            pl.BlockSpec((tm, d), lambda i: (i, gate_col0)),
            pl.BlockSpec((tm, d), lambda i: (i, gate_col0 + 1)),
            pl.BlockSpec((tm, d), lambda i: (i, gate_col0 + 2)),
            pl.BlockSpec((1, mod_rows, d), mod_map),
            pl.BlockSpec((1, mod_rows, d), mod_map),
            pl.BlockSpec((1, mod_rows, d), mod_map),
            full(n2), full(ws_bf), full(bs_tab), full(wbr_bf), full(wo_bf), full(wr), full(br),
        ],
        out_specs=[
            pl.BlockSpec((tm, d), lambda i: (i, 0)),
            pl.BlockSpec((tm, d), lambda i: (i, 0)),
            pl.BlockSpec((tm, LANES), lambda i: (i, 0)),
            pl.BlockSpec((8, LANES), lambda i: (0, 0)),
        ],
        out_shape=[jax.ShapeDtypeStruct((t, d), F32), jax.ShapeDtypeStruct((t, d), F32),
                   jax.ShapeDtypeStruct((t, LANES), F32), jax.ShapeDtypeStruct((8, LANES), F32)],
        compiler_params=_cparams(("arbitrary",)),
        name="merge",
    )(x, uv, uv, ob, oc, gates, gates, gates, gt1, sc2, sh2, n2, ws_bf, bs_tab, wbr_bf, wo_bf, wr, br)


def _gather_kernel(idx_ref, src_ref, o_ref, sem):
    rows = o_ref.shape[0]
    base = pl.program_id(0) * rows

    def row_copy(r, tok):
        return pltpu.make_async_copy(src_ref.at[pl.ds(tok, 1)], o_ref.at[pl.ds(r, 1)], sem)

    def issue(g, carry):
        for k in range(GATHER_UNROLL):
            r = g * GATHER_UNROLL + k
            row_copy(r, idx_ref[base + r]).start(priority=k % 2)
        return carry

    lax.fori_loop(0, rows // GATHER_UNROLL, issue, 0)

    def drain(r, carry):
        row_copy(r, 0).wait()
        return carry

    lax.fori_loop(0, rows, drain, 0, unroll=8)


def gather_call(idx, src, rows_per_step):
    n = idx.shape[0]
    d = src.shape[1]
    grid_spec = pltpu.PrefetchScalarGridSpec(
        num_scalar_prefetch=1,
        grid=(n // rows_per_step,),
        in_specs=[pl.BlockSpec(memory_space=pl.ANY)],
        out_specs=pl.BlockSpec((rows_per_step, d), lambda i, idx: (i, 0)),
        scratch_shapes=[pltpu.SemaphoreType.DMA(())],
    )
    return pl.pallas_call(
        _gather_kernel,
        grid_spec=grid_spec,
        out_shape=jax.ShapeDtypeStruct((n, d), src.dtype),
        compiler_params=_cparams(("arbitrary",)),
        name="row_gather",
    )(idx, src)


def _scatter_kernel(dest_ref, h_ref, init_ref, o_ref, sem):
    del init_ref
    tm = h_ref.shape[0]
    base = pl.program_id(0) * tm
    t_total = pl.num_programs(0) * tm

    def row_copy(r, dst):
        return pltpu.make_async_copy(h_ref.at[pl.ds(r, 1)], o_ref.at[pl.ds(dst, 1)], sem)

    def issue(g, carry):
        for k in range(GATHER_UNROLL):
            r = g * GATHER_UNROLL + k
            for slot in range(EXPERT_TOPK):
                row_copy(r, dest_ref[slot * t_total + base + r]).start(priority=slot)
        return carry

    lax.fori_loop(0, tm // GATHER_UNROLL, issue, 0)

    def drain(r, carry):
        row_copy(0, 0).wait()
        return carry

    lax.fori_loop(0, tm * EXPERT_TOPK, drain, 0, unroll=8)


def scatter_call(dest, h2, init, tm):
    t, d = h2.shape
    grid_spec = pltpu.PrefetchScalarGridSpec(
        num_scalar_prefetch=1,
        grid=(t // tm,),
        in_specs=[pl.BlockSpec((tm, d), lambda i, dest: (i, 0)), pl.BlockSpec(memory_space=pl.ANY)],
        out_specs=pl.BlockSpec(memory_space=pl.ANY),
        scratch_shapes=[pltpu.SemaphoreType.DMA(())],
    )
    return pl.pallas_call(
        _scatter_kernel,
        grid_spec=grid_spec,
        out_shape=jax.ShapeDtypeStruct(init.shape, init.dtype),
        input_output_aliases={2: 0},
        compiler_params=_cparams(("arbitrary",)),
        name="row_scatter",
    )(dest, h2, init)


def _expert_kernel(be_ref, nu_ref, x_ref, wg_ref, wu_ref, wd_ref, o_ref, wg_bf, wu_bf, wd_bf):
    i = pl.program_id(0)
    e = be_ref[i]
    prev = be_ref[jnp.maximum(i - 1, 0)]

    @pl.when((i == 0) | (e != prev))
    def _():
        wg_bf[...] = wg_ref[...].astype(BF16)
        wu_bf[...] = wu_ref[...].astype(BF16)
        wd_bf[...] = wd_ref[...].astype(BF16)

    @pl.when(i < nu_ref[0])
    def _():
        x = x_ref[...].astype(BF16)
        g = jnp.dot(x, wg_bf[...], preferred_element_type=F32)
        u = jnp.dot(x, wu_bf[...], preferred_element_type=F32)
        h = g * jax.nn.sigmoid(g) * u
        o_ref[...] = jnp.dot(h.astype(BF16), wd_bf[...], preferred_element_type=F32)

    @pl.when(i >= nu_ref[0])
    def _():
        o_ref[...] = jnp.zeros(o_ref.shape, o_ref.dtype)


def expert_call(blk_e, n_used, xs, wg, wu, wd, layer):
    n, d = xs.shape
    f = wg.shape[-1]
    nblk = n // MOE_ROWS
    grid_spec = pltpu.PrefetchScalarGridSpec(
        num_scalar_prefetch=2,
        grid=(nblk,),
        in_specs=[
            pl.BlockSpec((MOE_ROWS, d), lambda i, be, nu: (i, 0)),
            pl.BlockSpec((None, None, d, f), lambda i, be, nu: (layer, be[i], 0, 0)),
            pl.BlockSpec((None, None, d, f), lambda i, be, nu: (layer, be[i], 0, 0)),
            pl.BlockSpec((None, None, f, d), lambda i, be, nu: (layer, be[i], 0, 0)),
        ],
        out_specs=pl.BlockSpec((MOE_ROWS, d), lambda i, be, nu: (i, 0)),
        scratch_shapes=[pltpu.VMEM((d, f), BF16), pltpu.VMEM((d, f), BF16), pltpu.VMEM((f, d), BF16)],
    )
    return pl.pallas_call(
        _expert_kernel,
        grid_spec=grid_spec,
        out_shape=jax.ShapeDtypeStruct((n, d), F32),
        compiler_params=_cparams(("arbitrary",)),
        name="experts",
    )(blk_e, n_used, xs, wg, wu, wd)


def _combine_kernel(x1_ref, y0_ref, y1_ref, rt_ref, gt2_ref, fg_ref, o_ref, *, final):
    rt = rt_ref[...]
    x2 = x1_ref[...] + gt2_ref[0] * (rt[:, 2:3] * y0_ref[...] + rt[:, 3:4] * y1_ref[...])
    if final:
        x2 = x2 * lax.rsqrt(jnp.mean(x2 * x2, axis=-1, keepdims=True) + EPS) * fg_ref[...]
    o_ref[...] = x2


def combine_call(x1, ya, rt, gt2, fg, tm, seq_tiles, final):
    t, d = x1.shape
    mod_rows = gt2.shape[1]
    if mod_rows == 1:
        mod_map = lambda i: (i // seq_tiles, 0, 0)
    else:
        mod_map = lambda i: (0, 0, 0)
    nt = t // tm
    return pl.pallas_call(
        functools.partial(_combine_kernel, final=final),
        grid=(nt,),
        in_specs=[
            pl.BlockSpec((tm, d), lambda i: (i, 0)),
            pl.BlockSpec((tm, d), lambda i: (i, 0)),
            pl.BlockSpec((tm, d), lambda i: (i + nt, 0)),
            pl.BlockSpec((tm, LANES), lambda i: (i, 0)),
            pl.BlockSpec((1, mod_rows, d), mod_map),
            pl.BlockSpec((1, d), lambda i: (0, 0)),
        ],
        out_specs=pl.BlockSpec((tm, d), lambda i: (i, 0)),
        out_shape=jax.ShapeDtypeStruct((t, d), F32),
        compiler_params=_cparams(("arbitrary",)),
        name="combine",
    )(x1, ya, ya, rt, gt2, fg)


def moe_block(h2, x1, rt, cnt, gt2, fg, wg, wu, wd, layer, tm, seq_tiles, final):
    t, d = h2.shape
    a = t * EXPERT_TOPK
    eid = rt[:, :EXPERT_TOPK].astype(jnp.int32).T
    rank = rt[:, 4:4 + EXPERT_TOPK].astype(jnp.int32).T
    counts = cnt[0, :N_EXPERTS].astype(jnp.int32)
    padded = (counts + MOE_ROWS - 1) // MOE_ROWS * MOE_ROWS
    pad_ends = jnp.cumsum(padded)
    pad_starts = pad_ends - padded
    experts = jnp.arange(N_EXPERTS, dtype=jnp.int32)
    start_of = jnp.sum(jnp.where(eid[..., None] == experts, pad_starts, 0), axis=-1)
    dest = (start_of + rank).astype(jnp.int32).reshape(-1)
    nblk = -(-a // MOE_ROWS) + N_EXPERTS
    n_used = (pad_ends[-1] // MOE_ROWS).astype(jnp.int32)
    blk_start = jnp.arange(nblk, dtype=jnp.int32) * MOE_ROWS
    blk_e = jnp.sum((pad_ends[None, :] <= blk_start[:, None]).astype(jnp.int32), axis=1)
    blk_e = jnp.minimum(blk_e, N_EXPERTS - 1)
    last_e = blk_e[jnp.maximum(n_used - 1, 0)]
    blk_e = jnp.where(jnp.arange(nblk) < n_used, blk_e, last_e)

    xs = scatter_call(dest, h2, jnp.zeros((nblk * MOE_ROWS, d), h2.dtype), tm)
    yb = expert_call(blk_e, n_used.reshape(1), xs, wg, wu, wd, layer)
    rows = min(MOE_ROWS, a)
    ya = gather_call(dest, yb, rows)
    return combine_call(x1, ya, rt, gt2, fg, tm, seq_tiles, final)


def layer_prompt(x, mods, lw, layer, rope, b, s, lam_init, final, fg):
    t, d = x.shape
    bw = d // 2
    sh1, sc1, gt1, sh2, sc2, gt2 = mods
    tm_in = min(512, s)
    (uv, gates, mk, mv, dk, dv, mqt, dqt, mkb, dkb, mvt, dvt) = inproj_attn_call(
        x, sc1, sh1, lw['n1'], lw['win'], rope, lw['lng'], lw['lnb'], tm_in, b, s)
    ob_t = attn_call(mqt, mkb, mvt, 1, True)
    oc_t = attn_call(dqt, dkb, dvt, 2, False, lw['dlam'], lw['subg_col'], lam_init)
    tm = 256
    x1, h2, rt, cnt = merge_call(x, uv, gates, ob_t.reshape(b, bw, s), oc_t.reshape(b, bw, s), gt1, sc2, sh2,
                            lw['n2'], lw['ws'], lw['bs_tab'], lw['wbr'], lw['wo'], lw['wr'], lw['br'],
                            tm, s // tm, 0, 0, True)
    x2 = moe_block(h2, x1, rt, cnt, gt2, fg, lw['weg'], lw['weu'], lw['wed'], layer, tm, s // tm, final)
    return x2, (mk, mv, dk, dv)


def layer_sample(x, mods, lw, layer, rope, db, ds, lam_init, final, fg, caches, page_table):
    t, d = x.shape
    bw = d // 2
    sh1, sc1, gt1, sh2, sc2, gt2 = mods
    zt = inproj_call(x, sc1, sh1, lw['n1'], lw['win'], rope, lw['lng'], lw['lnb'], t, 1)
    scale = HEAD_DIM ** -0.5
    eye = jnp.eye(N_UNITS, dtype=F32)

    def qbd(col0):
        q = zt[:, col0:col0 + bw].reshape(db, ds, N_UNITS, HEAD_DIM) * scale
        return jnp.einsum('btud,uw->btuwd', q, eye).reshape(db, ds * N_UNITS, bw)

    def pad_new(col0):
        a = zt[:, col0:col0 + bw].reshape(db, ds, bw)
        return jnp.pad(a, ((0, 0), (0, NEW_ROWS - ds), (0, 0)))

    unit = np.arange(ds * N_UNITS) % N_UNITS
    col = np.arange(bw)
    om_moba = jnp.asarray((col[None, :] // HEAD_DIM == unit[:, None]).astype(np.float32))
    om_diff = jnp.asarray((col[None, :] // (2 * HEAD_DIM) == unit[:, None] // 2).astype(np.float32))
    ckt, cvt, cdkt, cdv = caches
    ob = sattn_call(page_table, qbd(2 * bw), ckt, cvt, pad_new(3 * bw), pad_new(4 * bw), om_moba,
                    lw['dlam'], lw['subg_row'], layer, True, lam_init, ds).reshape(t, bw)
    oc = sattn_call(page_table, qbd(5 * bw), cdkt, cdv, pad_new(6 * bw), pad_new(7 * bw), om_diff,
                    lw['dlam'], lw['subg_row'], layer, False, lam_init, ds).reshape(t, bw)
    x1, h2, rt, cnt = merge_call(x, zt, zt, ob, oc, gt1, sc2, sh2, lw['n2'], lw['ws_s'], lw['bs_tab_s'], lw['wbr'],
                            lw['wo'], lw['wr'], lw['br'], t, 1, 0, (8 * bw) // d, False)
    x2 = moe_block(h2, x1, rt, cnt, gt2, fg, lw['weg'], lw['weu'], lw['wed'], layer, t, 1, final)
    outs = (zt[:, 3 * bw:4 * bw], zt[:, 4 * bw:5 * bw], zt[:, 6 * bw:7 * bw], zt[:, 7 * bw:8 * bw],
            zt[:, bw:2 * bw])
    return x2, outs


def kernel(x_prompt, x_sample, cache_moba_k, cache_moba_v, cache_diff_k, cache_diff_v, page_table, c_prompt, c_sample, norm1_g, norm2_g, final_g, w_ada, b_ada, w_in, gm_ln_g, gm_ln_b, gm_ws, gm_bs, diff_lambda, diff_subln_g, w_branch, w_out, w_group, b_group, w_router, b_router, w_gate_e, w_up_e, w_down_e):
    b, s, d = x_prompt.shape
    db, ds, _ = x_sample.shape
    depth = w_in.shape[0]
    bw = d // 2
    gd = bw // GM_GROUPS
    npool, page = cache_moba_k.shape[1], cache_moba_k.shape[2]
    past_len = page_table.shape[1] * page
    ts = db * ds
    assert s % ATT_BLOCK == 0 and ts % GM_CHUNK == 0 and GM_CHUNK % ds == 0

    rope_p = rope_tables(jnp.arange(s, dtype=jnp.int32))
    rope_s = rope_tables(jnp.tile(past_len + jnp.arange(ds, dtype=jnp.int32), db))
    caches = (
        jnp.transpose(cache_moba_k, (0, 1, 3, 4, 2)).reshape(depth, npool, bw, page),
        jnp.transpose(cache_moba_v, (0, 1, 3, 4, 2)).reshape(depth, npool, bw, page),
        jnp.transpose(cache_diff_k, (0, 1, 3, 4, 5, 2)).reshape(depth, npool, bw, page),
        cache_diff_v.reshape(depth, npool, page * cache_diff_v.shape[3], cache_diff_v.shape[4]),
    )

    nc = b + db
    ncp = -(-nc // 8) * 8
    c_all = jnp.pad(jnp.concatenate([c_prompt, c_sample], axis=0), ((0, ncp - nc), (0, 0)))
    tril = jnp.tril(jnp.ones((GM_CHUNK, GM_CHUNK), F32))
    fg = final_g.reshape(1, d)

    xp = x_prompt.reshape(b * s, d)
    xs = x_sample.reshape(ts, d)
    outs_p, outs_s = [], []
    for l in range(depth):
        lam_init = 0.8 - 0.6 * math.exp(-0.3 * l)
        final = l == depth - 1
        m = ada_call(c_all, w_ada, b_ada, l)
        parts = [m[:, i * d:(i + 1) * d] for i in range(6)]
        mods_p = tuple(p[:b].reshape(b, 1, d) for p in parts)
        mods_s = tuple(jnp.repeat(p[b:nc], ds, axis=0).reshape(1, ts, d) for p in parts)

        ws_tril = gm_ws[l] * tril[None]
        ws_small = ws_tril[:, :ds, :ds]
        ws_s = jnp.einsum('ab,gts->gatbs', jnp.eye(GM_CHUNK // ds, dtype=F32), ws_small)
        ws_s = ws_s.reshape(GM_GROUPS, GM_CHUNK, GM_CHUNK)
        bs_tab = jnp.repeat(gm_bs[l].T, gd, axis=1)
        bs_tab_s = jnp.tile(bs_tab[:ds], (GM_CHUNK // ds, 1))
        wr = jnp.pad(jnp.concatenate([w_group[l], w_router[l]], axis=1),
                     ((0, 0), (0, LANES - N_GROUPS - N_EXPERTS)))
        wr_hi = wr.astype(BF16)
        wr = jnp.stack([wr_hi, (wr - wr_hi.astype(F32)).astype(BF16)])
        br = jnp.pad(jnp.concatenate([b_group[l], b_router[l]]), (0, LANES - N_GROUPS - N_EXPERTS))
        lw = dict(
            n1=norm1_g[l].reshape(1, d), n2=norm2_g[l].reshape(1, d),
            win=w_in[l].astype(BF16), lng=gm_ln_g[l].reshape(1, bw), lnb=gm_ln_b[l].reshape(1, bw),
            ws=ws_tril.astype(BF16), bs_tab=bs_tab, ws_s=ws_s.astype(BF16), bs_tab_s=bs_tab_s,
            dlam=diff_lambda[l], subg_col=diff_subln_g[l].reshape(-1, 1), subg_row=diff_subln_g[l].reshape(1, -1),
            wbr=w_branch[l].astype(BF16), wo=w_out[l].astype(BF16), wr=wr, br=br.reshape(1, LANES),
            weg=w_gate_e, weu=w_up_e, wed=w_down_e,
        )
        xp, op = layer_prompt(xp, mods_p, lw, l, rope_p, b, s, lam_init, final, fg)
        xs, os_ = layer_sample(xs, mods_s, lw, l, rope_s, db, ds, lam_init, final, fg, caches, page_table)
        outs_p.append(op)
        outs_s.append(os_)

    nh = N_UNITS
    stack = lambda lst, i, shape: jnp.stack([o[i].reshape(shape) for o in lst])
    tok_last = lambda i, shape: jnp.moveaxis(stack(outs_p, i, (b,) + shape + (s,)), -1, 2)
    return (
        xp.reshape(b, s, d), xs.reshape(db, ds, d),
        tok_last(0, (nh, HEAD_DIM)), tok_last(1, (nh, HEAD_DIM)),
        tok_last(2, (nh // 2, 2, HEAD_DIM)), stack(outs_p, 3, (b, s, nh // 2, 2 * HEAD_DIM)),
        stack(outs_s, 0, (db, ds, nh, HEAD_DIM)), stack(outs_s, 1, (db, ds, nh, HEAD_DIM)),
        stack(outs_s, 2, (db, ds, nh // 2, 2, HEAD_DIM)), stack(outs_s, 3, (db, ds, nh // 2, 2 * HEAD_DIM)),
        stack(outs_s, 4, (db, ds, bw)),
    )
```

```python
import functools
import math

import jax
import jax.numpy as jnp
import numpy as np
from jax import lax
from jax.experimental import pallas as pl
from jax.experimental.pallas import tpu as pltpu

F32 = jnp.float32
BF16 = jnp.bfloat16
HIGHEST = lax.Precision.HIGHEST

EPS = 1e-6
NEG = -1e30
BIG = 1e30

N_BRANCHES = 3
GM_GROUPS = 4
GM_CHUNK = 128
N_UNITS = 8
HEAD_DIM = 64
ROT = HEAD_DIM // 4
ROPE_THETA = 500000.0
ATT_BLOCK = 256
ATT_CHAINS = 8
ATT_BLOCKS_PER_ITER = 2
MOBA_TOPK = 3
SAMPLE_BLOCKS_PER_STEP = 8
NEW_ROWS = 16
N_GROUPS = 4
EXPERTS_PER_GROUP = 8
N_EXPERTS = N_GROUPS * EXPERTS_PER_GROUP
EXPERT_TOPK = 2
MOE_ROWS = 256
MOE_ROWS_SMALL = 64
GATHER_UNROLL = 8
LANES = 128
VMEM_LIMIT = 48 * 1024 * 1024
NT_DIMS = (((1,), (1,)), ((), ()))


def _cparams(sem):
    return pltpu.CompilerParams(dimension_semantics=sem, vmem_limit_bytes=VMEM_LIMIT)


def _ada_kernel(c_ref, w_ref, b_ref, o_ref):
    c = c_ref[...]
    a = c * jax.nn.sigmoid(c)
    o_ref[...] = jnp.dot(a, w_ref[...], preferred_element_type=F32, precision=HIGHEST) + b_ref[...]


def ada_call(c_all, w_ada, b_ada, layer):
    m, d = c_all.shape
    n = w_ada.shape[-1]
    tn = n // 4
    return pl.pallas_call(
        _ada_kernel,
        grid=(n // tn,),
        in_specs=[
            pl.BlockSpec((m, d), lambda j: (0, 0)),
            pl.BlockSpec((None, d, tn), lambda j: (layer, 0, j)),
            pl.BlockSpec((None, 1, tn), lambda j: (layer, 0, j)),
        ],
        out_specs=pl.BlockSpec((m, tn), lambda j: (0, j)),
        out_shape=jax.ShapeDtypeStruct((m, n), F32),
        compiler_params=_cparams(("arbitrary",)),
        name="ada",
    )(c_all, w_ada, b_ada.reshape(b_ada.shape[0], 1, n))


def _norm_mod(x_ref, sc_ref, sh_ref, g_ref):
    x = x_ref[...]
    y = x * lax.rsqrt(jnp.mean(x * x, axis=-1, keepdims=True) + EPS) * g_ref[...]
    return (y * (1.0 + sc_ref[0]) + sh_ref[0]).astype(BF16)


def _gelu_ln(z, lng_ref, lnb_ref):
    a = jax.nn.gelu(z)
    mu = jnp.mean(a, axis=-1, keepdims=True)
    ac = a - mu
    var = jnp.mean(ac * ac, axis=-1, keepdims=True)
    return ac * lax.rsqrt(var + EPS) * lng_ref[...] + lnb_ref[...]


def _rope(z, rc_ref, rs1_ref, rs2_ref):
    width = z.shape[-1]
    reps = width // LANES
    c = jnp.concatenate([rc_ref[...]] * reps, axis=1)
    s1 = jnp.concatenate([rs1_ref[...]] * reps, axis=1)
    s2 = jnp.concatenate([rs2_ref[...]] * reps, axis=1)
    up = pltpu.roll(z, width - ROT // 2, 1)
    dn = pltpu.roll(z, ROT // 2, 1)
    return z * c + up * s1 + dn * s2


def _inproj_kernel(x_ref, sc_ref, sh_ref, g_ref, w_ref, rc_ref, rs1_ref, rs2_ref, lng_ref, lnb_ref,
                   o_ref, h_scr):
    j = pl.program_id(1)

    @pl.when(j == 0)
    def _():
        h_scr[...] = _norm_mod(x_ref, sc_ref, sh_ref, g_ref)

    def proj():
        return jnp.dot(h_scr[...], w_ref[...], preferred_element_type=F32)

    @pl.when(j == 0)
    def _():
        o_ref[...] = jax.nn.gelu(proj())

    @pl.when(j == 1)
    def _():
        o_ref[...] = _gelu_ln(proj(), lng_ref, lnb_ref)

    @pl.when((j == 2) | (j == 3) | (j == 5) | (j == 6))
    def _():
        o_ref[...] = _rope(proj(), rc_ref, rs1_ref, rs2_ref)

    @pl.when((j == 4) | (j == 7))
    def _():
        o_ref[...] = proj()

    @pl.when(j >= 8)
    def _():
        o_ref[...] = jax.nn.sigmoid(proj())


def _inproj_attn_kernel(x_ref, sc_ref, sh_ref, g_ref, w_ref, rc_ref, rs1_ref, rs2_ref, lng_ref, lnb_ref,
                        uv_ref, gates_ref, mk_ref, mv_ref, dk_ref, dv_ref,
                        mqt_ref, dqt_ref, mkb_ref, dkb_ref, mvt_ref, dvt_ref, h_scr):
    j = pl.program_id(1)
    scale = HEAD_DIM ** -0.5 * math.log2(math.e)

    @pl.when(j == 0)
    def _():
        h_scr[...] = _norm_mod(x_ref, sc_ref, sh_ref, g_ref)

    def proj():
        return jnp.dot(h_scr[...], w_ref[...], preferred_element_type=F32)

    def roped():
        return _rope(proj(), rc_ref, rs1_ref, rs2_ref)

    def put_qt(ref, zr):
        zt = (zr * scale).T.astype(BF16)
        for u in range(ref.shape[0]):
            ref[u] = zt[u * HEAD_DIM:(u + 1) * HEAD_DIM, :]

    def put_kb(ref, zr):
        zb = zr.astype(BF16)
        for u in range(ref.shape[0]):
            for kb in range(ref.shape[1]):
                ref[u, kb] = zb[kb * ATT_BLOCK:(kb + 1) * ATT_BLOCK, u * HEAD_DIM:(u + 1) * HEAD_DIM]

    def put_vt(ref, zt):
        zb = zt.astype(BF16)
        dv = ref.shape[2]
        for n in range(ref.shape[0]):
            for kb in range(ref.shape[1]):
                ref[n, kb] = zb[n * dv:(n + 1) * dv, kb * ATT_BLOCK:(kb + 1) * ATT_BLOCK]

    @pl.when(j == 0)
    def _():
        uv_ref[...] = jax.nn.gelu(proj())

    @pl.when(j == 1)
    def _():
        uv_ref[...] = _gelu_ln(proj(), lng_ref, lnb_ref)

    @pl.when(j == 2)
    def _():
        put_qt(mqt_ref, roped())

    @pl.when(j == 3)
    def _():
        zr = roped()
        mk_ref[...] = zr.T
        put_kb(mkb_ref, zr)

    @pl.when(j == 4)
    def _():
        zt = proj().T
        mv_ref[...] = zt
        put_vt(mvt_ref, zt)

    @pl.when(j == 5)
    def _():
        put_qt(dqt_ref, roped())

    @pl.when(j == 6)
    def _():
        zr = roped()
        dk_ref[...] = zr.T
        put_kb(dkb_ref, zr)

    @pl.when(j == 7)
    def _():
        z = proj()
        nh = dv_ref.shape[0] // z.shape[0]
        for hh in range(nh):
            dv_ref[pl.ds(hh, z.shape[0], stride=nh), :] = z[:, hh * (2 * HEAD_DIM):(hh + 1) * (2 * HEAD_DIM)]
        put_vt(dvt_ref, z.T)

    @pl.when(j >= 8)
    def _():
        gates_ref[...] = jax.nn.sigmoid(proj())


def _inproj_in_specs(tm, d, tn, mod_rows, seq_tiles):
    if mod_rows == 1:
        mod_map = lambda i, j: (i // seq_tiles, 0, 0)
    else:
        mod_map = lambda i, j: (0, 0, 0)
    rope_map = lambda i, j: (i % seq_tiles, 0)
    return [
        pl.BlockSpec((tm, d), lambda i, j: (i, 0)),
        pl.BlockSpec((1, mod_rows, d), mod_map),
        pl.BlockSpec((1, mod_rows, d), mod_map),
        pl.BlockSpec((1, d), lambda i, j: (0, 0)),
        pl.BlockSpec((d, tn), lambda i, j: (0, j)),
        pl.BlockSpec((tm, LANES), rope_map),
        pl.BlockSpec((tm, LANES), rope_map),
        pl.BlockSpec((tm, LANES), rope_map),
        pl.BlockSpec((1, tn), lambda i, j: (0, 0)),
        pl.BlockSpec((1, tn), lambda i, j: (0, 0)),
    ]


def inproj_call(x, sc, sh, g, w_bf, rope, lng, lnb, tm, seq_tiles):
    t, d = x.shape
    n = w_bf.shape[1]
    tn = 512
    rc, rs1, rs2 = rope
    return pl.pallas_call(
        _inproj_kernel,
        grid=(t // tm, n // tn),
        in_specs=_inproj_in_specs(tm, d, tn, sc.shape[1], seq_tiles),
        out_specs=pl.BlockSpec((tm, tn), lambda i, j: (i, j)),
        out_shape=jax.ShapeDtypeStruct((t, n), F32),
        scratch_shapes=[pltpu.VMEM((tm, d), BF16)],
        compiler_params=_cparams(("arbitrary", "arbitrary")),
        name="inproj",
    )(x, sc, sh, g, w_bf, rc, rs1, rs2, lng, lnb)


def inproj_attn_call(x, sc, sh, g, w_bf, rope, lng, lnb, tm, b, s):
    t, d = x.shape
    tn = 512
    n = w_bf.shape[1]
    seq_tiles = s // tm
    nb = s // ATT_BLOCK
    kbt = tm // ATT_BLOCK
    nu, dh = N_UNITS, HEAD_DIM
    rc, rs1, rs2 = rope
    bi = lambda i: i // seq_tiles
    si = lambda i: i % seq_tiles
    nvh = nu // 2
    tr_spec = pl.BlockSpec((None, tn, tm), lambda i, j: (bi(i), 0, si(i)))
    qt_spec =pl.BlockSpec((None, nu, dh, tm), lambda i, j: (bi(i), 0, 0, si(i)))
    kb_spec = pl.BlockSpec((None, nu, kbt, ATT_BLOCK, dh), lambda i, j: (bi(i), 0, si(i), 0, 0))

    def vt_spec(nv):
        return pl.BlockSpec((None, nv, kbt, tn // nv, ATT_BLOCK), lambda i, j: (bi(i), 0, si(i), 0, 0))

    sds = jax.ShapeDtypeStruct
    return pl.pallas_call(
        _inproj_attn_kernel,
        grid=(t // tm, n // tn),
        in_specs=_inproj_in_specs(tm, d, tn, sc.shape[1], seq_tiles),
        out_specs=[
            pl.BlockSpec((tm, tn), lambda i, j: (i, jnp.minimum(j, 1))),
            pl.BlockSpec((tm, tn), lambda i, j: (i, jnp.maximum(j - 8, 0))),
            tr_spec, tr_spec, tr_spec, pl.BlockSpec((tm * nvh, tn // nvh), lambda i, j: (i, 0)),
            qt_spec, qt_spec, kb_spec, kb_spec, vt_spec(nu), vt_spec(nu // 2),
        ],
        out_shape=[
            sds((t, 2 * tn), F32), sds((t, n - 8 * tn), F32),
            sds((b, tn, s), F32), sds((b, tn, s), F32), sds((b, tn, s), F32),
            sds((t * nvh, tn // nvh), F32),
            sds((b, nu, dh, s), BF16), sds((b, nu, dh, s), BF16),
            sds((b, nu, nb, ATT_BLOCK, dh), BF16), sds((b, nu, nb, ATT_BLOCK, dh), BF16),
            sds((b, nu, nb, dh, ATT_BLOCK), BF16), sds((b, nu // 2, nb, 2 * dh, ATT_BLOCK), BF16),
        ],
        scratch_shapes=[pltpu.VMEM((tm, d), BF16)],
        compiler_params=_cparams(("arbitrary", "arbitrary")),
        name="inproj_attn",
    )(x, sc, sh, g, w_bf, rc, rs1, rs2, lng, lnb)


def rope_tables(pos):
    half = ROT // 2
    inv = ROPE_THETA ** (-(jnp.arange(half, dtype=F32) * 2.0 / ROT))
    ang = pos.astype(F32)[:, None] * inv[None, :]
    cos, sin = jnp.cos(ang), jnp.sin(ang)
    ones = jnp.ones((pos.shape[0], HEAD_DIM - ROT), F32)
    zeros8 = jnp.zeros((pos.shape[0], half), F32)
    zrest = jnp.zeros_like(ones)
    c = jnp.concatenate([cos, cos, ones], axis=1)
    s1 = jnp.concatenate([-sin, zeros8, zrest], axis=1)
    s2 = jnp.concatenate([zeros8, sin, zrest], axis=1)
    tile = lambda a: jnp.concatenate([a] * (LANES // HEAD_DIM), axis=1)
    return tile(c), tile(s1), tile(s2)


def _diff_lambda(dl_ref, lam_init):
    dl = dl_ref[...]
    return (jnp.exp(jnp.sum(dl[0:1] * dl[1:2], axis=1, keepdims=True))
            - jnp.exp(jnp.sum(dl[2:3] * dl[3:4], axis=1, keepdims=True)) + lam_init)


def _attn_kernel(*refs, nmaps, use_sel, lam_init):
    if use_sel:
        q_ref, k_ref, v_ref, o_ref, kmean_scr, sel_scr = refs
    else:
        q_ref, k_ref, v_ref, dl_ref, sg_ref, o_ref = refs
    qi = pl.program_id(2)
    nchain = q_ref.shape[0]
    nb = k_ref.shape[1]
    tq = q_ref.shape[-1]
    blk = k_ref.shape[2]

    if use_sel:
        @pl.when(qi == 0)
        def _():
            for c in range(nchain):
                kmean_scr[c] = jnp.mean(k_ref[c].astype(F32), axis=1)

        row = lax.broadcasted_iota(jnp.int32, (nb, tq), 0)
        for c in range(nchain):
            gate = jnp.dot(kmean_scr[c], q_ref[c].astype(F32), preferred_element_type=F32,
                           precision=HIGHEST)
            past = row < qi
            gate = jnp.where(past, gate, NEG)
            sel = jnp.zeros((nb, tq), F32)
            for _ in range(MOBA_TOPK):
                top = jnp.max(gate, axis=0, keepdims=True)
                first = jnp.min(jnp.where(gate == top, row, nb), axis=0, keepdims=True)
                hit = row == first
                sel = jnp.where(hit & past, 1.0, sel)
                gate = jnp.where(hit, -jnp.inf, gate)
            sel_scr[c] = sel

    kr = lax.broadcasted_iota(jnp.int32, (blk, tq), 0)
    qc = lax.broadcasted_iota(jnp.int32, (blk, tq), 1)
    qts = [q_ref[c] for c in range(nchain)]
    ss = [jnp.dot(k_ref[c, qi], qts[c], preferred_element_type=F32) for c in range(nchain)]
    stats = []
    for c in range(nchain):
        s = jnp.where(kr <= qc, ss[c], NEG)
        m0 = jnp.max(s, axis=0, keepdims=True)
        p = jnp.exp2(s - m0)
        stats.append((m0, jnp.sum(p, axis=0, keepdims=True), p.astype(BF16)))
    carry0 = tuple(
        (stats[c][0], stats[c][1],
         jnp.dot(v_ref[c // nmaps, qi], stats[c][2], preferred_element_type=F32))
        for c in range(nchain))

    def block_step(kjs, carry):
        ss = [[jnp.dot(k_ref[c, kj], qts[c], preferred_element_type=F32) for kj in kjs]
              for c in range(nchain)]
        upd = []
        for c in range(nchain):
            m, l, _ = carry[c]
            m_new = m
            ons = []
            for i, kj in enumerate(kjs):
                smax = jnp.max(ss[c][i], axis=0, keepdims=True)
                if use_sel:
                    on = sel_scr[c, pl.ds(kj, 1), :] > 0.5
                    ons.append(on)
                    m_new = jnp.where(on, jnp.maximum(m_new, smax), m_new)
                else:
                    m_new = jnp.maximum(m_new, smax)
            alpha = jnp.exp2(m - m_new)
            l_new = alpha * l
            ps = []
            for i in range(len(kjs)):
                m_use = jnp.where(ons[i], m_new, BIG) if use_sel else m_new
                p = jnp.exp2(ss[c][i] - m_use)
                l_new = l_new + jnp.sum(p, axis=0, keepdims=True)
                ps.append(p.astype(BF16))
            upd.append((m_new, l_new, alpha, ps))
        out = []
        for c in range(nchain):
            acc = upd[c][2] * carry[c][2]
            for i, kj in enumerate(kjs):
                acc = acc + jnp.dot(v_ref[c // nmaps, kj], upd[c][3][i], preferred_element_type=F32)
            out.append((upd[c][0], upd[c][1], acc))
        return tuple(out)

    npair = qi // ATT_BLOCKS_PER_ITER
    carry = lax.fori_loop(
        0, npair,
        lambda t, cr: block_step([ATT_BLOCKS_PER_ITER * t + i for i in range(ATT_BLOCKS_PER_ITER)], cr),
        carry0)
    carry = lax.fori_loop(npair * ATT_BLOCKS_PER_ITER, qi, lambda kj, cr: block_step([kj], cr), carry)
    outs = [acc / l for (_, l, acc) in carry]

    if use_sel:
        for c in range(nchain):
            o_ref[c] = outs[c]
    else:
        lam = _diff_lambda(dl_ref, lam_init)
        for hh in range(nchain // 2):
            o = outs[2 * hh] - lam * outs[2 * hh + 1]
            ms = jnp.mean(o * o, axis=0, keepdims=True)
            o_ref[hh] = o * lax.rsqrt(ms + EPS) * sg_ref[...] * (1.0 - lam_init)


def attn_call(qt, k, vt, nmaps, use_sel, dlam=None, subg=None, lam_init=0.0):
    b, nu, dh, s = qt.shape
    nb, blk = k.shape[2], k.shape[3]
    nv, dv = vt.shape[1], vt.shape[3]
    tq = blk
    nc = ATT_CHAINS
    nvb = nc // nmaps
    kern = functools.partial(_attn_kernel, nmaps=nmaps, use_sel=use_sel, lam_init=lam_init)
    in_specs = [
        pl.BlockSpec((None, nc, dh, tq), lambda bi, n, qi: (bi, n, 0, qi)),
        pl.BlockSpec((None, nc, nb, blk, dh), lambda bi, n, qi: (bi, n, 0, 0, 0)),
        pl.BlockSpec((None, nvb, nb, dv, blk), lambda bi, n, qi: (bi, n, 0, 0, 0)),
    ]
    args = [qt, k, vt]
    scratch = []
    if use_sel:
        scratch = [pltpu.VMEM((nc, nb, dh), F32), pltpu.VMEM((nc, nb, tq), F32)]
    else:
        in_specs += [pl.BlockSpec(dlam.shape, lambda bi, n, qi: (0, 0)),
                     pl.BlockSpec(subg.shape, lambda bi, n, qi: (0, 0))]
        args += [dlam, subg]
    return pl.pallas_call(
        kern,
        grid=(b, nu // nc, s // tq),
        in_specs=in_specs,
        out_specs=pl.BlockSpec((None, nvb, dv, tq), lambda bi, n, qi: (bi, n, 0, qi)),
        out_shape=jax.ShapeDtypeStruct((b, nv, dv, s), F32),
        scratch_shapes=scratch,
        compiler_params=_cparams(("arbitrary", "arbitrary", "arbitrary")),
        name="moba_attn" if use_sel else "diff_attn",
    )(*args)


def _sattn_kernel(pt_ref, *refs, use_sel, nblk, bps, n_new, lam_init):
    del pt_ref
    npg = 2 * bps
    qbd_ref = refs[0]
    kt_refs = refs[1:1 + npg]
    v_refs = refs[1 + npg:1 + 2 * npg]
    kn_ref, vn_ref, om_ref, dl_ref, sg_ref, o_ref, opart, g_scr, m_scr, l_scr = refs[1 + 2 * npg:]
    j = pl.program_id(1)
    qf = qbd_ref[0]
    rows, wdt = qf.shape
    q_hi = qf.astype(BF16)
    q_lo = (qf - q_hi.astype(F32)).astype(BF16)
    lhs = jnp.concatenate([q_hi, q_lo], axis=0)
    lane = lax.broadcasted_iota(jnp.int32, (rows, LANES), 1)
    om = om_ref[...]

    @pl.when(j == 0)
    def _():
        g_scr[...] = jnp.full((rows, LANES), NEG, F32)
        m_scr[...] = jnp.full((rows, LANES), NEG, F32)
        l_scr[...] = jnp.zeros((rows, LANES), F32)

    def scores(kt):
        s2 = jnp.dot(lhs, kt, preferred_element_type=F32)
        return s2[:rows] + s2[rows:]

    gacc, macc, lacc = g_scr[...], m_scr[...], l_scr[...]
    ss = [jnp.concatenate([scores(kt_refs[2 * bb + w][...].astype(BF16)) for w in range(2)], axis=1)
          for bb in range(bps)]
    stats = []
    for bb in range(bps):
        mj = jnp.max(ss[bb], axis=-1, keepdims=True)
        p = jnp.exp(ss[bb] - mj)
        stats.append((mj, jnp.sum(p, axis=-1, keepdims=True), p.astype(BF16)))
    for bb in range(bps):
        blk_id = j * bps + bb
        s = ss[bb]
        mj, lj, pb = stats[bb]
        oj = None
        for w in range(2):
            pw = pb[:, w * LANES:(w + 1) * LANES]
            vref = v_refs[2 * bb + w]
            if use_sel:
                ow = lax.dot_general(pw, vref[...].astype(BF16), NT_DIMS, preferred_element_type=F32)
            else:
                nh = vref.shape[0] // LANES
                ow = jnp.concatenate(
                    [jnp.dot(pw, vref[pl.ds(hh, LANES, stride=nh), :].astype(BF16),
                             preferred_element_type=F32) for hh in range(nh)], axis=1)
            oj = ow if oj is None else oj + ow
        opart[blk_id] = oj * om
        hit = lane == blk_id
        if use_sel:
            gacc = jnp.where(hit, jnp.sum(s, axis=-1, keepdims=True), gacc)
        macc = jnp.where(hit, mj, macc)
        lacc = jnp.where(hit, lj, lacc)
    g_scr[...] = gacc
    m_scr[...] = macc
    l_scr[...] = lacc

    @pl.when(j == nblk // bps - 1)
    def _():
        zpad = jnp.zeros((LANES - NEW_ROWS, wdt), BF16)
        kn = jnp.concatenate([kn_ref[0].astype(BF16), zpad], axis=0)
        vn = jnp.concatenate([vn_ref[0].astype(BF16), zpad], axis=0)
        so2 = lax.dot_general(lhs, kn, NT_DIMS, preferred_element_type=F32)
        so = so2[:rows] + so2[rows:]
        rtok = lax.broadcasted_iota(jnp.int32, (rows, LANES), 0) // N_UNITS
        so = jnp.where((lane <= rtok) & (lane < n_new), so, NEG)
        mo = jnp.max(so, axis=-1, keepdims=True)
        po = jnp.exp(so - mo)
        lo = jnp.sum(po, axis=-1, keepdims=True)
        oo = jnp.dot(po.astype(BF16), vn, preferred_element_type=F32) * om

        inb = lane < nblk
        if use_sel:
            g = jnp.where(inb, gacc, -jnp.inf)
            selm = jnp.zeros((rows, LANES), F32)
            for _ in range(min(MOBA_TOPK, nblk)):
                top = jnp.max(g, axis=-1, keepdims=True)
                first = jnp.min(jnp.where(g == top, lane, LANES), axis=-1, keepdims=True)
                hit = lane == first
                selm = jnp.where(hit, 1.0, selm)
                g = jnp.where(hit, -jnp.inf, g)
            selb = selm > 0.5
        else:
            selb = inb
        mt = jnp.maximum(jnp.max(jnp.where(selb, macc, NEG), axis=-1, keepdims=True), mo)
        w = jnp.where(selb, jnp.exp(macc - mt), 0.0)
        wo = jnp.exp(mo - mt)
        ltot = jnp.sum(w * lacc, axis=-1, keepdims=True) + wo * lo
        acc = wo * oo
        for jb in range(nblk):
            acc = acc + w[:, jb:jb + 1] * opart[jb]
        res = acc / ltot
        if not use_sel:
            lam = _diff_lambda(dl_ref, lam_init)
            rr = lax.broadcasted_iota(jnp.int32, (rows, 1), 0)
            res = res * jnp.where(rr % 2 == 0, 1.0, -lam)
        out = jnp.sum(res.reshape(rows // N_UNITS, N_UNITS, wdt), axis=1)
        if not use_sel:
            dv = sg_ref.shape[-1]
            segs = []
            for hh in range(wdt // dv):
                seg = out[:, hh * dv:(hh + 1) * dv]
                ms = jnp.mean(seg * seg, axis=-1, keepdims=True)
                segs.append(seg * lax.rsqrt(ms + EPS) * sg_ref[...] * (1.0 - lam_init))
            out = jnp.concatenate(segs, axis=1)
        o_ref[0] = out


def sattn_call(page_table, qbd, cache_kt, cache_v, knew, vnew, omask, dlam, subg, layer, use_sel,
               lam_init, n_new):
    db, rows, w = qbd.shape
    page = cache_kt.shape[3]
    n_pages = page_table.shape[1]
    nblk = n_pages * page // ATT_BLOCK
    bps = math.gcd(nblk, SAMPLE_BLOCKS_PER_STEP)
    assert ATT_BLOCK == 2 * page and page == LANES and nblk <= LANES and nblk % bps == 0
    pt_flat = page_table.reshape(-1)

    def page_spec(arr, which):
        return pl.BlockSpec((None, None) + arr.shape[2:],
                            lambda b, j, pt: (layer, pt[b * n_pages + 2 * bps * j + which], 0, 0))

    kern = functools.partial(_sattn_kernel, use_sel=use_sel, nblk=nblk, bps=bps, n_new=n_new,
                             lam_init=lam_init)
    n_tok = rows // N_UNITS
    grid_spec = pltpu.PrefetchScalarGridSpec(
        num_scalar_prefetch=1,
        grid=(db, nblk // bps),
        in_specs=(
            [pl.BlockSpec((1, rows, w), lambda b, j, pt: (b, 0, 0))]
            + [page_spec(cache_kt, i) for i in range(2 * bps)]
            + [page_spec(cache_v, i) for i in range(2 * bps)]
            + [pl.BlockSpec((1, NEW_ROWS, w), lambda b, j, pt: (b, 0, 0)),
               pl.BlockSpec((1, NEW_ROWS, w), lambda b, j, pt: (b, 0, 0)),
               pl.BlockSpec((rows, w), lambda b, j, pt: (0, 0)),
               pl.BlockSpec(dlam.shape, lambda b, j, pt: (0, 0)),
               pl.BlockSpec(subg.shape, lambda b, j, pt: (0, 0))]
        ),
        out_specs=pl.BlockSpec((1, n_tok, w), lambda b, j, pt: (b, 0, 0)),
        scratch_shapes=[pltpu.VMEM((nblk, rows, w), F32), pltpu.VMEM((rows, LANES), F32),
                        pltpu.VMEM((rows, LANES), F32), pltpu.VMEM((rows, LANES), F32)],
    )
    return pl.pallas_call(
        kern,
        grid_spec=grid_spec,
        out_shape=jax.ShapeDtypeStruct((db, n_tok, w), F32),
        compiler_params=_cparams(("arbitrary", "arbitrary")),
        name="moba_sample" if use_sel else "diff_sample",
    )(pt_flat, qbd, *([cache_kt] * (2 * bps)), *([cache_v] * (2 * bps)), knew, vnew, omask, dlam, subg)


def _merge_kernel(x_ref, u_ref, v_ref, ob_ref, oc_ref, ga_ref, gb_ref, gc_ref, gt1_ref, sc2_ref,
                  sh2_ref, n2_ref, ws_ref, bs_ref, wbr_ref, wo_ref, wr_ref, br_ref,
                  x1_ref, h2_ref, rt_ref, cnt_ref, *, o_transposed):
    tm = x_ref.shape[0]
    gd = ws_ref.shape[-1]
    v = v_ref[...].astype(BF16)
    chunks = []
    for c in range(tm // GM_CHUNK):
        r0 = c * GM_CHUNK
        cols = [jnp.dot(ws_ref[g], v[r0:r0 + GM_CHUNK, g * gd:(g + 1) * gd], preferred_element_type=F32)
                for g in range(GM_GROUPS)]
        chunks.append(jnp.concatenate(cols, axis=1) + bs_ref[...])
    mixed = jnp.concatenate(chunks, axis=0) if len(chunks) > 1 else chunks[0]
    o_a = u_ref[...] * mixed
    if o_transposed:
        o_b = ob_ref[...].T
        o_c = oc_ref[...].T
    else:
        o_b = ob_ref[...]
        o_c = oc_ref[...]
    merged = (ga_ref[...] * jnp.dot(o_a.astype(BF16), wbr_ref[0], preferred_element_type=F32)
              + gb_ref[...] * jnp.dot(o_b.astype(BF16), wbr_ref[1], preferred_element_type=F32)
              + gc_ref[...] * jnp.dot(o_c.astype(BF16), wbr_ref[2], preferred_element_type=F32))
    y = jnp.dot(merged.astype(BF16), wo_ref[...], preferred_element_type=F32)
    x1 = x_ref[...] + gt1_ref[0] * y
    x1_ref[...] = x1
    h2 = x1 * lax.rsqrt(jnp.mean(x1 * x1, axis=-1, keepdims=True) + EPS) * n2_ref[...]
    h2 = h2 * (1.0 + sc2_ref[0]) + sh2_ref[0]
    h2_ref[...] = h2

    h_hi = h2.astype(BF16)
    h_lo = (h2 - h_hi.astype(F32)).astype(BF16)
    logits = (jnp.dot(h_hi, wr_ref[0], preferred_element_type=F32)
              + jnp.dot(h_lo, wr_ref[0], preferred_element_type=F32)
              + jnp.dot(h_hi, wr_ref[1], preferred_element_type=F32)) + br_ref[...]
    lane = lax.broadcasted_iota(jnp.int32, logits.shape, 1)
    gl = jnp.where(lane < N_GROUPS, logits, NEG)
    gmax = jnp.max(gl, axis=-1, keepdims=True)
    gsel = jnp.min(jnp.where(gl == gmax, lane, LANES), axis=-1, keepdims=True)
    gw = 1.0 / jnp.sum(jnp.exp(gl - gmax), axis=-1, keepdims=True)
    lo = N_GROUPS + EXPERTS_PER_GROUP * gsel
    el = jnp.where((lane >= lo) & (lane < lo + EXPERTS_PER_GROUP), logits, NEG)
    v1 = jnp.max(el, axis=-1, keepdims=True)
    i1 = jnp.min(jnp.where(el == v1, lane, LANES), axis=-1, keepdims=True)
    el2 = jnp.where(lane == i1, NEG, el)
    v2 = jnp.max(el2, axis=-1, keepdims=True)
    i2 = jnp.min(jnp.where(el2 == v2, lane, LANES), axis=-1, keepdims=True)
    e2 = jnp.exp(v2 - v1)
    w1 = gw / (1.0 + e2)
    w2 = gw * e2 / (1.0 + e2)

    @pl.when(pl.program_id(0) == 0)
    def _():
        cnt_ref[...] = jnp.zeros(cnt_ref.shape, F32)

    e1 = i1 - N_GROUPS
    e2i = i2 - N_GROUPS
    onehot = jnp.where((lane == e1) | (lane == e2i), 1.0, 0.0)
    rr = lax.broadcasted_iota(jnp.int32, (tm, tm), 0)
    cc = lax.broadcasted_iota(jnp.int32, (tm, tm), 1)
    lower = jnp.where(rr > cc, 1.0, 0.0).astype(BF16)
    before = jnp.dot(lower, onehot.astype(BF16), preferred_element_type=F32) + cnt_ref[0:1, :]
    r1 = jnp.sum(jnp.where(lane == e1, before, 0.0), axis=-1, keepdims=True)
    r2 = jnp.sum(jnp.where(lane == e2i, before, 0.0), axis=-1, keepdims=True)
    cnt_ref[...] = cnt_ref[...] + jnp.sum(onehot, axis=0, keepdims=True)

    vals = (e1.astype(F32), e2i.astype(F32), w1, w2, r1, r2)
    rt = jnp.zeros(logits.shape, F32)
    for k, val in enumerate(vals):
        rt = jnp.where(lane == k, val, rt)
    rt_ref[...] = rt


def merge_call(x, uv, gates, ob, oc, gt1, sc2, sh2, n2, ws_bf, bs_tab, wbr_bf, wo_bf, wr, br, tm,
               seq_tiles, uv_col0, gate_col0, o_transposed):
    t, d = x.shape
    bw = d // 2
    mod_rows = gt1.shape[1]
    if mod_rows == 1:
        mod_map = lambda i: (i // seq_tiles, 0, 0)
    else:
        mod_map = lambda i: (0, 0, 0)
    if o_transposed:
        o_spec = pl.BlockSpec((None, bw, tm), lambda i: (i // seq_tiles, 0, i % seq_tiles))
    else:
        o_spec = pl.BlockSpec((tm, bw), lambda i: (i, 0))
    full = lambda a: pl.BlockSpec(a.shape, lambda i: (0,) * a.ndim)
    return pl.pallas_call(
        functools.partial(_merge_kernel, o_transposed=o_transposed),
        grid=(t // tm,),
        in_specs=[
            pl.BlockSpec((tm, d), lambda i: (i, 0)),
            pl.BlockSpec((tm, bw), lambda i: (i, uv_col0)),
            pl.BlockSpec((tm, bw), lambda i: (i, uv_col0 + 1)),
            o_spec, o_spec,
            pl.BlockSpec((tm, d), lambda i: (i, gate_col0)),
            pl.BlockSpec((tm, d), lambda i: (i, gate_col0 + 1)),
            pl.BlockSpec((tm, d), lambda i: (i, gate_col0 + 2)),
            pl.BlockSpec((1, mod_rows, d), mod_map),
            pl.BlockSpec((1, mod_rows, d), mod_map),
            pl.BlockSpec((1, mod_rows, d), mod_map),
            full(n2), full(ws_bf), full(bs_tab), full(wbr_bf), full(wo_bf), full(wr), full(br),
        ],
        out_specs=[
            pl.BlockSpec((tm, d), lambda i: (i, 0)),
            pl.BlockSpec((tm, d), lambda i: (i, 0)),
            pl.BlockSpec((tm, LANES), lambda i: (i, 0)),
            pl.BlockSpec((8, LANES), lambda i: (0, 0)),
        ],
        out_shape=[jax.ShapeDtypeStruct((t, d), F32), jax.ShapeDtypeStruct((t, d), F32),
                   jax.ShapeDtypeStruct((t, LANES), F32), jax.ShapeDtypeStruct((8, LANES), F32)],
        compiler_params=_cparams(("arbitrary",)),
        name="merge",
    )(x, uv, uv, ob, oc, gates, gates, gates, gt1, sc2, sh2, n2, ws_bf, bs_tab, wbr_bf, wo_bf, wr, br)


def _gather_kernel(idx_ref, src_ref, o_ref, sem):
    rows = o_ref.shape[0]
    base = pl.program_id(0) * rows

    def row_copy(r, tok):
        return pltpu.make_async_copy(src_ref.at[pl.ds(tok, 1)], o_ref.at[pl.ds(r, 1)], sem)

    def issue(g, carry):
        for k in range(GATHER_UNROLL):
            r = g * GATHER_UNROLL + k
            row_copy(r, idx_ref[base + r]).start(priority=k % 2)
        return carry

    lax.fori_loop(0, rows // GATHER_UNROLL, issue, 0)

    def drain(r, carry):
        row_copy(r, 0).wait()
        return carry

    lax.fori_loop(0, rows, drain, 0, unroll=8)


def gather_call(idx, src, rows_per_step):
    n = idx.shape[0]
    d = src.shape[1]
    grid_spec = pltpu.PrefetchScalarGridSpec(
        num_scalar_prefetch=1,
        grid=(n // rows_per_step,),
        in_specs=[pl.BlockSpec(memory_space=pl.ANY)],
        out_specs=pl.BlockSpec((rows_per_step, d), lambda i, idx: (i, 0)),
        scratch_shapes=[pltpu.SemaphoreType.DMA(())],
    )
    return pl.pallas_call(
        _gather_kernel,
        grid_spec=grid_spec,
        out_shape=jax.ShapeDtypeStruct((n, d), src.dtype),
        compiler_params=_cparams(("arbitrary",)),
        name="row_gather",
    )(idx, src)


def _scatter_kernel(dest_ref, h_ref, init_ref, o_ref, sem):
    del init_ref
    tm = h_ref.shape[0]
    base = pl.program_id(0) * tm
    t_total = pl.num_programs(0) * tm

    def row_copy(r, dst):
        return pltpu.make_async_copy(h_ref.at[pl.ds(r, 1)], o_ref.at[pl.ds(dst, 1)], sem)

    def issue(g, carry):
        for k in range(GATHER_UNROLL):
            r = g * GATHER_UNROLL + k
            for slot in range(EXPERT_TOPK):
                row_copy(r, dest_ref[slot * t_total + base + r]).start(priority=slot)
        return carry

    lax.fori_loop(0, tm // GATHER_UNROLL, issue, 0)

    def drain(r, carry):
        row_copy(0, 0).wait()
        return carry

    lax.fori_loop(0, tm * EXPERT_TOPK, drain, 0, unroll=8)


def scatter_call(dest, h2, init, tm):
    t, d = h2.shape
    grid_spec = pltpu.PrefetchScalarGridSpec(
        num_scalar_prefetch=1,
        grid=(t // tm,),
        in_specs=[pl.BlockSpec((tm, d), lambda i, dest: (i, 0)), pl.BlockSpec(memory_space=pl.ANY)],
        out_specs=pl.BlockSpec(memory_space=pl.ANY),
        scratch_shapes=[pltpu.SemaphoreType.DMA(())],
    )
    return pl.pallas_call(
        _scatter_kernel,
        grid_spec=grid_spec,
        out_shape=jax.ShapeDtypeStruct(init.shape, init.dtype),
        input_output_aliases={2: 0},
        compiler_params=_cparams(("arbitrary",)),
        name="row_scatter",
    )(dest, h2, init)


def _expert_kernel(be_ref, nu_ref, x_ref, wg_ref, wu_ref, wd_ref, o_ref, wg_bf, wu_bf, wd_bf):
    i = pl.program_id(0)
    e = be_ref[i]
    prev = be_ref[jnp.maximum(i - 1, 0)]

    @pl.when((i == 0) | (e != prev))
    def _():
        wg_bf[...] = wg_ref[...].astype(BF16)
        wu_bf[...] = wu_ref[...].astype(BF16)
        wd_bf[...] = wd_ref[...].astype(BF16)

    @pl.when(i < nu_ref[0])
    def _():
        x = x_ref[...].astype(BF16)
        g = jnp.dot(x, wg_bf[...], preferred_element_type=F32)
        u = jnp.dot(x, wu_bf[...], preferred_element_type=F32)
        h = g * jax.nn.sigmoid(g) * u
        o_ref[...] = jnp.dot(h.astype(BF16), wd_bf[...], preferred_element_type=F32)

    @pl.when(i >= nu_ref[0])
    def _():
        o_ref[...] = jnp.zeros(o_ref.shape, o_ref.dtype)


def expert_call(blk_e, n_used, xs, wg, wu, wd, layer, rows):
    n, d = xs.shape
    f = wg.shape[-1]
    nblk = n // rows
    grid_spec = pltpu.PrefetchScalarGridSpec(
        num_scalar_prefetch=2,
        grid=(nblk,),
        in_specs=[
            pl.BlockSpec((rows, d), lambda i, be, nu: (i, 0)),
            pl.BlockSpec((None, None, d, f), lambda i, be, nu: (layer, be[i], 0, 0)),
            pl.BlockSpec((None, None, d, f), lambda i, be, nu: (layer, be[i], 0, 0)),
            pl.BlockSpec((None, None, f, d), lambda i, be, nu: (layer, be[i], 0, 0)),
        ],
        out_specs=pl.BlockSpec((rows, d), lambda i, be, nu: (i, 0)),
        scratch_shapes=[pltpu.VMEM((d, f), BF16), pltpu.VMEM((d, f), BF16), pltpu.VMEM((f, d), BF16)],
    )
    return pl.pallas_call(
        _expert_kernel,
        grid_spec=grid_spec,
        out_shape=jax.ShapeDtypeStruct((n, d), F32),
        compiler_params=_cparams(("arbitrary",)),
        name="experts",
    )(blk_e, n_used, xs, wg, wu, wd)


def _combine_kernel(x1_ref, y0_ref, y1_ref, rt_ref, gt2_ref, fg_ref, o_ref, *, final):
    rt = rt_ref[...]
    x2 = x1_ref[...] + gt2_ref[0] * (rt[:, 2:3] * y0_ref[...] + rt[:, 3:4] * y1_ref[...])
    if final:
        x2 = x2 * lax.rsqrt(jnp.mean(x2 * x2, axis=-1, keepdims=True) + EPS) * fg_ref[...]
    o_ref[...] = x2


def combine_call(x1, ya, rt, gt2, fg, tm, seq_tiles, final):
    t, d = x1.shape
    mod_rows = gt2.shape[1]
    if mod_rows == 1:
        mod_map = lambda i: (i // seq_tiles, 0, 0)
    else:
        mod_map = lambda i: (0, 0, 0)
    nt = t // tm
    return pl.pallas_call(
        functools.partial(_combine_kernel, final=final),
        grid=(nt,),
        in_specs=[
            pl.BlockSpec((tm, d), lambda i: (i, 0)),
            pl.BlockSpec((tm, d), lambda i: (i, 0)),
            pl.BlockSpec((tm, d), lambda i: (i + nt, 0)),
            pl.BlockSpec((tm, LANES), lambda i: (i, 0)),
            pl.BlockSpec((1, mod_rows, d), mod_map),
            pl.BlockSpec((1, d), lambda i: (0, 0)),
        ],
        out_specs=pl.BlockSpec((tm, d), lambda i: (i, 0)),
        out_shape=jax.ShapeDtypeStruct((t, d), F32),
        compiler_params=_cparams(("arbitrary",)),
        name="combine",
    )(x1, ya, ya, rt, gt2, fg)


def moe_block(h2, x1, rt, cnt, gt2, fg, wg, wu, wd, layer, tm, seq_tiles, final):
    t, d = h2.shape
    a = t * EXPERT_TOPK
    eid = rt[:, :EXPERT_TOPK].astype(jnp.int32).T
    rank = rt[:, 4:4 + EXPERT_TOPK].astype(jnp.int32).T
    counts = cnt[0, :N_EXPERTS].astype(jnp.int32)
    br = MOE_ROWS if a >= MOE_ROWS * N_EXPERTS else MOE_ROWS_SMALL
    padded = (counts + br - 1) // br * br
    pad_ends = jnp.cumsum(padded)
    pad_starts = pad_ends - padded
    experts = jnp.arange(N_EXPERTS, dtype=jnp.int32)
    start_of = jnp.sum(jnp.where(eid[..., None] == experts, pad_starts, 0), axis=-1)
    dest = (start_of + rank).astype(jnp.int32).reshape(-1)
    nblk = -(-a // br) + N_EXPERTS
    n_used = (pad_ends[-1] // br).astype(jnp.int32)
    blk_start = jnp.arange(nblk, dtype=jnp.int32) * br
    blk_e = jnp.sum((pad_ends[None, :] <= blk_start[:, None]).astype(jnp.int32), axis=1)
    blk_e = jnp.minimum(blk_e, N_EXPERTS - 1)
    last_e = blk_e[jnp.maximum(n_used - 1, 0)]
    blk_e = jnp.where(jnp.arange(nblk) < n_used, blk_e, last_e)

    xs = scatter_call(dest, h2, jnp.zeros((nblk * br, d), h2.dtype), tm)
    yb = expert_call(blk_e, n_used.reshape(1), xs, wg, wu, wd, layer, br)
    rows = min(MOE_ROWS, a)
    ya = gather_call(dest, yb, rows)
    return combine_call(x1, ya, rt, gt2, fg, tm, seq_tiles, final)


def layer_prompt(x, mods, lw, layer, rope, b, s, lam_init, final, fg):
    t, d = x.shape
    bw = d // 2
    sh1, sc1, gt1, sh2, sc2, gt2 = mods
    tm_in = min(512, s)
    (uv, gates, mk, mv, dk, dv, mqt, dqt, mkb, dkb, mvt, dvt) = inproj_attn_call(
        x, sc1, sh1, lw['n1'], lw['win'], rope, lw['lng'], lw['lnb'], tm_in, b, s)
    ob_t = attn_call(mqt, mkb, mvt, 1, True)
    oc_t = attn_call(dqt, dkb, dvt, 2, False, lw['dlam'], lw['subg_col'], lam_init)
    tm = 256
    x1, h2, rt, cnt = merge_call(x, uv, gates, ob_t.reshape(b, bw, s), oc_t.reshape(b, bw, s), gt1, sc2, sh2,
                            lw['n2'], lw['ws'], lw['bs_tab'], lw['wbr'], lw['wo'], lw['wr'], lw['br'],
                            tm, s // tm, 0, 0, True)
    x2 = moe_block(h2, x1, rt, cnt, gt2, fg, lw['weg'], lw['weu'], lw['wed'], layer, tm, s // tm, final)
    return x2, (mk, mv, dk, dv)


def layer_sample(x, mods, lw, layer, rope, db, ds, lam_init, final, fg, caches, page_table):
    t, d = x.shape
    bw = d // 2
    sh1, sc1, gt1, sh2, sc2, gt2 = mods
    zt = inproj_call(x, sc1, sh1, lw['n1'], lw['win'], rope, lw['lng'], lw['lnb'], t, 1)
    scale = HEAD_DIM ** -0.5
    eye = jnp.eye(N_UNITS, dtype=F32)

    def qbd(col0):
        q = zt[:, col0:col0 + bw].reshape(db, ds, N_UNITS, HEAD_DIM) * scale
        return jnp.einsum('btud,uw->btuwd', q, eye).reshape(db, ds * N_UNITS, bw)

    def pad_new(col0):
        a = zt[:, col0:col0 + bw].reshape(db, ds, bw)
        return jnp.pad(a, ((0, 0), (0, NEW_ROWS - ds), (0, 0)))

    unit = np.arange(ds * N_UNITS) % N_UNITS
    col = np.arange(bw)
    om_moba = jnp.asarray((col[None, :] // HEAD_DIM == unit[:, None]).astype(np.float32))
    om_diff = jnp.asarray((col[None, :] // (2 * HEAD_DIM) == unit[:, None] // 2).astype(np.float32))
    ckt, cvt, cdkt, cdv = caches
    ob = sattn_call(page_table, qbd(2 * bw), ckt, cvt, pad_new(3 * bw), pad_new(4 * bw), om_moba,
                    lw['dlam'], lw['subg_row'], layer, True, lam_init, ds).reshape(t, bw)
    oc = sattn_call(page_table, qbd(5 * bw), cdkt, cdv, pad_new(6 * bw), pad_new(7 * bw), om_diff,
                    lw['dlam'], lw['subg_row'], layer, False, lam_init, ds).reshape(t, bw)
    x1, h2, rt, cnt = merge_call(x, zt, zt, ob, oc, gt1, sc2, sh2, lw['n2'], lw['ws_s'], lw['bs_tab_s'], lw['wbr'],
                            lw['wo'], lw['wr'], lw['br'], t, 1, 0, (8 * bw) // d, False)
    x2 = moe_block(h2, x1, rt, cnt, gt2, fg, lw['weg'], lw['weu'], lw['wed'], layer, t, 1, final)
    outs = (zt[:, 3 * bw:4 * bw], zt[:, 4 * bw:5 * bw], zt[:, 6 * bw:7 * bw], zt[:, 7 * bw:8 * bw],
            zt[:, bw:2 * bw])
    return x2, outs


def kernel(x_prompt, x_sample, cache_moba_k, cache_moba_v, cache_diff_k, cache_diff_v, page_table, c_prompt, c_sample, norm1_g, norm2_g, final_g, w_ada, b_ada, w_in, gm_ln_g, gm_ln_b, gm_ws, gm_bs, diff_lambda, diff_subln_g, w_branch, w_out, w_group, b_group, w_router, b_router, w_gate_e, w_up_e, w_down_e):
    b, s, d = x_prompt.shape
    db, ds, _ = x_sample.shape
    depth = w_in.shape[0]
    bw = d // 2
    gd = bw // GM_GROUPS
    npool, page = cache_moba_k.shape[1], cache_moba_k.shape[2]
    past_len = page_table.shape[1] * page
    ts = db * ds
    assert s % ATT_BLOCK == 0 and ts % GM_CHUNK == 0 and GM_CHUNK % ds == 0

    rope_p = rope_tables(jnp.arange(s, dtype=jnp.int32))
    rope_s = rope_tables(jnp.tile(past_len + jnp.arange(ds, dtype=jnp.int32), db))
    caches = (
        jnp.transpose(cache_moba_k, (0, 1, 3, 4, 2)).reshape(depth, npool, bw, page),
        jnp.transpose(cache_moba_v, (0, 1, 3, 4, 2)).reshape(depth, npool, bw, page),
        jnp.transpose(cache_diff_k, (0, 1, 3, 4, 5, 2)).reshape(depth, npool, bw, page),
        cache_diff_v.reshape(depth, npool, page * cache_diff_v.shape[3], cache_diff_v.shape[4]),
    )

    nc = b + db
    ncp = -(-nc // 8) * 8
    c_all = jnp.pad(jnp.concatenate([c_prompt, c_sample], axis=0), ((0, ncp - nc), (0, 0)))
    tril = jnp.tril(jnp.ones((GM_CHUNK, GM_CHUNK), F32))
    fg = final_g.reshape(1, d)

    xp = x_prompt.reshape(b * s, d)
    xs = x_sample.reshape(ts, d)
    outs_p, outs_s = [], []
    for l in range(depth):
        lam_init = 0.8 - 0.6 * math.exp(-0.3 * l)
        final = l == depth - 1
        m = ada_call(c_all, w_ada, b_ada, l)
        parts = [m[:, i * d:(i + 1) * d] for i in range(6)]
        mods_p = tuple(p[:b].reshape(b, 1, d) for p in parts)
        mods_s = tuple(jnp.repeat(p[b:nc], ds, axis=0).reshape(1, ts, d) for p in parts)

        ws_tril = gm_ws[l] * tril[None]
        ws_small = ws_tril[:, :ds, :ds]
        ws_s = jnp.einsum('ab,gts->gatbs', jnp.eye(GM_CHUNK // ds, dtype=F32), ws_small)
        ws_s = ws_s.reshape(GM_GROUPS, GM_CHUNK, GM_CHUNK)
        bs_tab = jnp.repeat(gm_bs[l].T, gd, axis=1)
        bs_tab_s = jnp.tile(bs_tab[:ds], (GM_CHUNK // ds, 1))
        wr = jnp.pad(jnp.concatenate([w_group[l], w_router[l]], axis=1),
                     ((0, 0), (0, LANES - N_GROUPS - N_EXPERTS)))
        wr_hi = wr.astype(BF16)
        wr = jnp.stack([wr_hi, (wr - wr_hi.astype(F32)).astype(BF16)])
        br = jnp.pad(jnp.concatenate([b_group[l], b_router[l]]), (0, LANES - N_GROUPS - N_EXPERTS))
        lw = dict(
            n1=norm1_g[l].reshape(1, d), n2=norm2_g[l].reshape(1, d),
            win=w_in[l].astype(BF16), lng=gm_ln_g[l].reshape(1, bw), lnb=gm_ln_b[l].reshape(1, bw),
            ws=ws_tril.astype(BF16), bs_tab=bs_tab, ws_s=ws_s.astype(BF16), bs_tab_s=bs_tab_s,
            dlam=diff_lambda[l], subg_col=diff_subln_g[l].reshape(-1, 1), subg_row=diff_subln_g[l].reshape(1, -1),
            wbr=w_branch[l].astype(BF16), wo=w_out[l].astype(BF16), wr=wr, br=br.reshape(1, LANES),
            weg=w_gate_e, weu=w_up_e, wed=w_down_e,
        )
        xp, op = layer_prompt(xp, mods_p, lw, l, rope_p, b, s, lam_init, final, fg)
        xs, os_ = layer_sample(xs, mods_s, lw, l, rope_s, db, ds, lam_init, final, fg, caches, page_table)
        outs_p.append(op)
        outs_s.append(os_)

    nh = N_UNITS
    stack = lambda lst, i, shape: jnp.stack([o[i].reshape(shape) for o in lst])
    tok_last = lambda i, shape: jnp.moveaxis(stack(outs_p, i, (b,) + shape + (s,)), -1, 2)
    return (
        xp.reshape(b, s, d), xs.reshape(db, ds, d),
        tok_last(0, (nh, HEAD_DIM)), tok_last(1, (nh, HEAD_DIM)),
        tok_last(2, (nh // 2, 2, HEAD_DIM)), stack(outs_p, 3, (b, s, nh // 2, 2 * HEAD_DIM)),
        stack(outs_s, 0, (db, ds, nh, HEAD_DIM)), stack(outs_s, 1, (db, ds, nh, HEAD_DIM)),
        stack(outs_s, 2, (db, ds, nh // 2, 2, HEAD_DIM)), stack(outs_s, 3, (db, ds, nh // 2, 2 * HEAD_DIM)),
        stack(outs_s, 4, (db, ds, bw)),
    )
```

```python
import functools
import math

import jax
import jax.numpy as jnp
import numpy as np
from jax import lax
from jax.experimental import pallas as pl
from jax.experimental.pallas import tpu as pltpu

F32 = jnp.float32
BF16 = jnp.bfloat16
HIGHEST = lax.Precision.HIGHEST

EPS = 1e-6
NEG = -1e30
BIG = 1e30

N_BRANCHES = 3
GM_GROUPS = 4
GM_CHUNK = 128
N_UNITS = 8
HEAD_DIM = 64
ROT = HEAD_DIM // 4
ROPE_THETA = 500000.0
ATT_BLOCK = 256
ATT_CHAINS = 8
ATT_BLOCKS_PER_ITER = 2
MOBA_TOPK = 3
SAMPLE_BLOCKS_PER_STEP = 8
NEW_ROWS = 16
N_GROUPS = 4
EXPERTS_PER_GROUP = 8
N_EXPERTS = N_GROUPS * EXPERTS_PER_GROUP
EXPERT_TOPK = 2
MOE_ROWS = 256
MOE_ROWS_SMALL = 64
GATHER_UNROLL = 8
LANES = 128
VMEM_LIMIT = 48 * 1024 * 1024
NT_DIMS = (((1,), (1,)), ((), ()))


def _cparams(sem):
    return pltpu.CompilerParams(dimension_semantics=sem, vmem_limit_bytes=VMEM_LIMIT)


def _ada_kernel(c_ref, w_ref, b_ref, o_ref):
    c = c_ref[...]
    a = c * jax.nn.sigmoid(c)
    o_ref[...] = jnp.dot(a, w_ref[...], preferred_element_type=F32, precision=HIGHEST) + b_ref[...]


def ada_call(c_all, w_ada, b_ada, layer):
    m, d = c_all.shape
    n = w_ada.shape[-1]
    tn = n // 4
    return pl.pallas_call(
        _ada_kernel,
        grid=(n // tn,),
        in_specs=[
            pl.BlockSpec((m, d), lambda j: (0, 0)),
            pl.BlockSpec((None, d, tn), lambda j: (layer, 0, j)),
            pl.BlockSpec((None, 1, tn), lambda j: (layer, 0, j)),
        ],
        out_specs=pl.BlockSpec((m, tn), lambda j: (0, j)),
        out_shape=jax.ShapeDtypeStruct((m, n), F32),
        compiler_params=_cparams(("arbitrary",)),
        name="ada",
    )(c_all, w_ada, b_ada.reshape(b_ada.shape[0], 1, n))


def _norm_mod(x_ref, sc_ref, sh_ref, g_ref):
    x = x_ref[...]
    y = x * lax.rsqrt(jnp.mean(x * x, axis=-1, keepdims=True) + EPS) * g_ref[...]
    return (y * (1.0 + sc_ref[0]) + sh_ref[0]).astype(BF16)


def _gelu_ln(z, lng_ref, lnb_ref):
    a = jax.nn.gelu(z)
    mu = jnp.mean(a, axis=-1, keepdims=True)
    ac = a - mu
    var = jnp.mean(ac * ac, axis=-1, keepdims=True)
    return ac * lax.rsqrt(var + EPS) * lng_ref[...] + lnb_ref[...]


def _rope(z, rc_ref, rs1_ref, rs2_ref):
    width = z.shape[-1]
    reps = width // LANES
    c = jnp.concatenate([rc_ref[...]] * reps, axis=1)
    s1 = jnp.concatenate([rs1_ref[...]] * reps, axis=1)
    s2 = jnp.concatenate([rs2_ref[...]] * reps, axis=1)
    up = pltpu.roll(z, width - ROT // 2, 1)
    dn = pltpu.roll(z, ROT // 2, 1)
    return z * c + up * s1 + dn * s2


def _inproj_kernel(x_ref, sc_ref, sh_ref, g_ref, w_ref, rc_ref, rs1_ref, rs2_ref, lng_ref, lnb_ref,
                   o_ref, h_scr):
    j = pl.program_id(1)

    @pl.when(j == 0)
    def _():
        h_scr[...] = _norm_mod(x_ref, sc_ref, sh_ref, g_ref)

    def proj():
        return jnp.dot(h_scr[...], w_ref[...], preferred_element_type=F32)

    @pl.when(j == 0)
    def _():
        o_ref[...] = jax.nn.gelu(proj())

    @pl.when(j == 1)
    def _():
        o_ref[...] = _gelu_ln(proj(), lng_ref, lnb_ref)

    @pl.when((j == 2) | (j == 3) | (j == 5) | (j == 6))
    def _():
        o_ref[...] = _rope(proj(), rc_ref, rs1_ref, rs2_ref)

    @pl.when((j == 4) | (j == 7))
    def _():
        o_ref[...] = proj()

    @pl.when(j >= 8)
    def _():
        o_ref[...] = jax.nn.sigmoid(proj())


def _inproj_attn_kernel(x_ref, sc_ref, sh_ref, g_ref, w_ref, rc_ref, rs1_ref, rs2_ref, lng_ref, lnb_ref,
                        uv_ref, gates_ref, mk_ref, mv_ref, dk_ref, dv_ref,
                        mqt_ref, dqt_ref, mkb_ref, dkb_ref, mvt_ref, dvt_ref, h_scr):
    j = pl.program_id(1)
    scale = HEAD_DIM ** -0.5 * math.log2(math.e)

    @pl.when(j == 0)
    def _():
        h_scr[...] = _norm_mod(x_ref, sc_ref, sh_ref, g_ref)

    def proj():
        return jnp.dot(h_scr[...], w_ref[...], preferred_element_type=F32)

    def roped():
        return _rope(proj(), rc_ref, rs1_ref, rs2_ref)

    def put_qt(ref, zr):
        zt = (zr * scale).T.astype(BF16)
        for u in range(ref.shape[0]):
            ref[u] = zt[u * HEAD_DIM:(u + 1) * HEAD_DIM, :]

    def put_kb(ref, zr):
        zb = zr.astype(BF16)
        for u in range(ref.shape[0]):
            for kb in range(ref.shape[1]):
                ref[u, kb] = zb[kb * ATT_BLOCK:(kb + 1) * ATT_BLOCK, u * HEAD_DIM:(u + 1) * HEAD_DIM]

    def put_vt(ref, zt):
        zb = zt.astype(BF16)
        dv = ref.shape[2]
        for n in range(ref.shape[0]):
            for kb in range(ref.shape[1]):
                ref[n, kb] = zb[n * dv:(n + 1) * dv, kb * ATT_BLOCK:(kb + 1) * ATT_BLOCK]

    @pl.when(j == 0)
    def _():
        uv_ref[...] = jax.nn.gelu(proj())

    @pl.when(j == 1)
    def _():
        uv_ref[...] = _gelu_ln(proj(), lng_ref, lnb_ref)

    @pl.when(j == 2)
    def _():
        put_qt(mqt_ref, roped())

    @pl.when(j == 3)
    def _():
        zr = roped()
        mk_ref[...] = zr.T
        put_kb(mkb_ref, zr)

    @pl.when(j == 4)
    def _():
        zt = proj().T
        mv_ref[...] = zt
        put_vt(mvt_ref, zt)

    @pl.when(j == 5)
    def _():
        put_qt(dqt_ref, roped())

    @pl.when(j == 6)
    def _():
        zr = roped()
        dk_ref[...] = zr.T
        put_kb(dkb_ref, zr)

    @pl.when(j == 7)
    def _():
        z = proj()
        nh = dv_ref.shape[0] // z.shape[0]
        for hh in range(nh):
            dv_ref[pl.ds(hh, z.shape[0], stride=nh), :] = z[:, hh * (2 * HEAD_DIM):(hh + 1) * (2 * HEAD_DIM)]
        put_vt(dvt_ref, z.T)

    @pl.when(j >= 8)
    def _():
        gates_ref[...] = jax.nn.sigmoid(proj())


def _inproj_in_specs(tm, d, tn, mod_rows, seq_tiles):
    if mod_rows == 1:
        mod_map = lambda i, j: (i // seq_tiles, 0, 0)
    else:
        mod_map = lambda i, j: (0, 0, 0)
    rope_map = lambda i, j: (i % seq_tiles, 0)
    return [
        pl.BlockSpec((tm, d), lambda i, j: (i, 0)),
        pl.BlockSpec((1, mod_rows, d), mod_map),
        pl.BlockSpec((1, mod_rows, d), mod_map),
        pl.BlockSpec((1, d), lambda i, j: (0, 0)),
        pl.BlockSpec((d, tn), lambda i, j: (0, j)),
        pl.BlockSpec((tm, LANES), rope_map),
        pl.BlockSpec((tm, LANES), rope_map),
        pl.BlockSpec((tm, LANES), rope_map),
        pl.BlockSpec((1, tn), lambda i, j: (0, 0)),
        pl.BlockSpec((1, tn), lambda i, j: (0, 0)),
    ]


def inproj_call(x, sc, sh, g, w_bf, rope, lng, lnb, tm, seq_tiles):
    t, d = x.shape
    n = w_bf.shape[1]
    tn = 512
    rc, rs1, rs2 = rope
    return pl.pallas_call(
        _inproj_kernel,
        grid=(t // tm, n // tn),
        in_specs=_inproj_in_specs(tm, d, tn, sc.shape[1], seq_tiles),
        out_specs=pl.BlockSpec((tm, tn), lambda i, j: (i, j)),
        out_shape=jax.ShapeDtypeStruct((t, n), F32),
        scratch_shapes=[pltpu.VMEM((tm, d), BF16)],
        compiler_params=_cparams(("arbitrary", "arbitrary")),
        name="inproj",
    )(x, sc, sh, g, w_bf, rc, rs1, rs2, lng, lnb)


def inproj_attn_call(x, sc, sh, g, w_bf, rope, lng, lnb, tm, b, s):
    t, d = x.shape
    tn = 512
    n = w_bf.shape[1]
    seq_tiles = s // tm
    nb = s // ATT_BLOCK
    kbt = tm // ATT_BLOCK
    nu, dh = N_UNITS, HEAD_DIM
    rc, rs1, rs2 = rope
    bi = lambda i: i // seq_tiles
    si = lambda i: i % seq_tiles
    nvh = nu // 2
    tr_spec = pl.BlockSpec((None, tn, tm), lambda i, j: (bi(i), 0, si(i)))
    qt_spec =pl.BlockSpec((None, nu, dh, tm), lambda i, j: (bi(i), 0, 0, si(i)))
    kb_spec = pl.BlockSpec((None, nu, kbt, ATT_BLOCK, dh), lambda i, j: (bi(i), 0, si(i), 0, 0))

    def vt_spec(nv):
        return pl.BlockSpec((None, nv, kbt, tn // nv, ATT_BLOCK), lambda i, j: (bi(i), 0, si(i), 0, 0))

    sds = jax.ShapeDtypeStruct
    return pl.pallas_call(
        _inproj_attn_kernel,
        grid=(t // tm, n // tn),
        in_specs=_inproj_in_specs(tm, d, tn, sc.shape[1], seq_tiles),
        out_specs=[
            pl.BlockSpec((tm, tn), lambda i, j: (i, jnp.minimum(j, 1))),
            pl.BlockSpec((tm, tn), lambda i, j: (i, jnp.maximum(j - 8, 0))),
            tr_spec, tr_spec, tr_spec, pl.BlockSpec((tm * nvh, tn // nvh), lambda i, j: (i, 0)),
            qt_spec, qt_spec, kb_spec, kb_spec, vt_spec(nu), vt_spec(nu // 2),
        ],
        out_shape=[
            sds((t, 2 * tn), F32), sds((t, n - 8 * tn), F32),
            sds((b, tn, s), F32), sds((b, tn, s), F32), sds((b, tn, s), F32),
            sds((t * nvh, tn // nvh), F32),
            sds((b, nu, dh, s), BF16), sds((b, nu, dh, s), BF16),
            sds((b, nu, nb, ATT_BLOCK, dh), BF16), sds((b, nu, nb, ATT_BLOCK, dh), BF16),
            sds((b, nu, nb, dh, ATT_BLOCK), BF16), sds((b, nu // 2, nb, 2 * dh, ATT_BLOCK), BF16),
        ],
        scratch_shapes=[pltpu.VMEM((tm, d), BF16)],
        compiler_params=_cparams(("arbitrary", "arbitrary")),
        name="inproj_attn",
    )(x, sc, sh, g, w_bf, rc, rs1, rs2, lng, lnb)


def rope_tables(pos):
    half = ROT // 2
    inv = ROPE_THETA ** (-(jnp.arange(half, dtype=F32) * 2.0 / ROT))
    ang = pos.astype(F32)[:, None] * inv[None, :]
    cos, sin = jnp.cos(ang), jnp.sin(ang)
    ones = jnp.ones((pos.shape[0], HEAD_DIM - ROT), F32)
    zeros8 = jnp.zeros((pos.shape[0], half), F32)
    zrest = jnp.zeros_like(ones)
    c = jnp.concatenate([cos, cos, ones], axis=1)
    s1 = jnp.concatenate([-sin, zeros8, zrest], axis=1)
    s2 = jnp.concatenate([zeros8, sin, zrest], axis=1)
    tile = lambda a: jnp.concatenate([a] * (LANES // HEAD_DIM), axis=1)
    return tile(c), tile(s1), tile(s2)


def _diff_lambda(dl_ref, lam_init):
    dl = dl_ref[...]
    return (jnp.exp(jnp.sum(dl[0:1] * dl[1:2], axis=1, keepdims=True))
            - jnp.exp(jnp.sum(dl[2:3] * dl[3:4], axis=1, keepdims=True)) + lam_init)


def _attn_kernel(*refs, nmaps, use_sel, lam_init):
    if use_sel:
        q_ref, k_ref, v_ref, o_ref, kmean_scr, sel_scr = refs
    else:
        q_ref, k_ref, v_ref, dl_ref, sg_ref, o_ref = refs
    qi = pl.program_id(2)
    nchain = q_ref.shape[0]
    nb = k_ref.shape[1]
    tq = q_ref.shape[-1]
    blk = k_ref.shape[2]

    if use_sel:
        @pl.when(qi == 0)
        def _():
            for c in range(nchain):
                kmean_scr[c] = jnp.mean(k_ref[c].astype(F32), axis=1)

        row = lax.broadcasted_iota(jnp.int32, (nb, tq), 0)
        for c in range(nchain):
            gate = jnp.dot(kmean_scr[c], q_ref[c].astype(F32), preferred_element_type=F32,
                           precision=HIGHEST)
            past = row < qi
            gate = jnp.where(past, gate, NEG)
            sel = jnp.zeros((nb, tq), F32)
            for _ in range(MOBA_TOPK):
                top = jnp.max(gate, axis=0, keepdims=True)
                first = jnp.min(jnp.where(gate == top, row, nb), axis=0, keepdims=True)
                hit = row == first
                sel = jnp.where(hit & past, 1.0, sel)
                gate = jnp.where(hit, -jnp.inf, gate)
            sel_scr[c] = sel

    kr = lax.broadcasted_iota(jnp.int32, (blk, tq), 0)
    qc = lax.broadcasted_iota(jnp.int32, (blk, tq), 1)
    qts = [q_ref[c] for c in range(nchain)]
    ss = [jnp.dot(k_ref[c, qi], qts[c], preferred_element_type=F32) for c in range(nchain)]
    stats = []
    for c in range(nchain):
        s = jnp.where(kr <= qc, ss[c], NEG)
        m0 = jnp.max(s, axis=0, keepdims=True)
        p = jnp.exp2(s - m0)
        stats.append((m0, jnp.sum(p, axis=0, keepdims=True), p.astype(BF16)))
    carry0 = tuple(
        (stats[c][0], stats[c][1],
         jnp.dot(v_ref[c // nmaps, qi], stats[c][2], preferred_element_type=F32))
        for c in range(nchain))

    def block_step(kjs, carry):
        ss = [[jnp.dot(k_ref[c, kj], qts[c], preferred_element_type=F32) for kj in kjs]
              for c in range(nchain)]
        upd = []
        for c in range(nchain):
            m, l, _ = carry[c]
            m_new = m
            ons = []
            for i, kj in enumerate(kjs):
                smax = jnp.max(ss[c][i], axis=0, keepdims=True)
                if use_sel:
                    on = sel_scr[c, pl.ds(kj, 1), :] > 0.5
                    ons.append(on)
                    m_new = jnp.where(on, jnp.maximum(m_new, smax), m_new)
                else:
                    m_new = jnp.maximum(m_new, smax)
            alpha = jnp.exp2(m - m_new)
            l_new = alpha * l
            ps = []
            for i in range(len(kjs)):
                m_use = jnp.where(ons[i], m_new, BIG) if use_sel else m_new
                p = jnp.exp2(ss[c][i] - m_use)
                l_new = l_new + jnp.sum(p, axis=0, keepdims=True)
                ps.append(p.astype(BF16))
            upd.append((m_new, l_new, alpha, ps))
        out = []
        for c in range(nchain):
            acc = upd[c][2] * carry[c][2]
            for i, kj in enumerate(kjs):
                acc = acc + jnp.dot(v_ref[c // nmaps, kj], upd[c][3][i], preferred_element_type=F32)
            out.append((upd[c][0], upd[c][1], acc))
        return tuple(out)

    npair = qi // ATT_BLOCKS_PER_ITER
    carry = lax.fori_loop(
        0, npair,
        lambda t, cr: block_step([ATT_BLOCKS_PER_ITER * t + i for i in range(ATT_BLOCKS_PER_ITER)], cr),
        carry0)
    carry = lax.fori_loop(npair * ATT_BLOCKS_PER_ITER, qi, lambda kj, cr: block_step([kj], cr), carry)
    outs = [acc / l for (_, l, acc) in carry]

    if use_sel:
        for c in range(nchain):
            o_ref[c] = outs[c]
    else:
        lam = _diff_lambda(dl_ref, lam_init)
        for hh in range(nchain // 2):
            o = outs[2 * hh] - lam * outs[2 * hh + 1]
            ms = jnp.mean(o * o, axis=0, keepdims=True)
            o_ref[hh] = o * lax.rsqrt(ms + EPS) * sg_ref[...] * (1.0 - lam_init)


def attn_call(qt, k, vt, nmaps, use_sel, dlam=None, subg=None, lam_init=0.0):
    b, nu, dh, s = qt.shape
    nb, blk = k.shape[2], k.shape[3]
    nv, dv = vt.shape[1], vt.shape[3]
    tq = blk
    nc = ATT_CHAINS
    nvb = nc // nmaps
    kern = functools.partial(_attn_kernel, nmaps=nmaps, use_sel=use_sel, lam_init=lam_init)
    in_specs = [
        pl.BlockSpec((None, nc, dh, tq), lambda bi, n, qi: (bi, n, 0, qi)),
        pl.BlockSpec((None, nc, nb, blk, dh), lambda bi, n, qi: (bi, n, 0, 0, 0)),
        pl.BlockSpec((None, nvb, nb, dv, blk), lambda bi, n, qi: (bi, n, 0, 0, 0)),
    ]
    args = [qt, k, vt]
    scratch = []
    if use_sel:
        scratch = [pltpu.VMEM((nc, nb, dh), F32), pltpu.VMEM((nc, nb, tq), F32)]
    else:
        in_specs += [pl.BlockSpec(dlam.shape, lambda bi, n, qi: (0, 0)),
                     pl.BlockSpec(subg.shape, lambda bi, n, qi: (0, 0))]
        args += [dlam, subg]
    return pl.pallas_call(
        kern,
        grid=(b, nu // nc, s // tq),
        in_specs=in_specs,
        out_specs=pl.BlockSpec((None, nvb, dv, tq), lambda bi, n, qi: (bi, n, 0, qi)),
        out_shape=jax.ShapeDtypeStruct((b, nv, dv, s), F32),
        scratch_shapes=scratch,
        compiler_params=_cparams(("arbitrary", "arbitrary", "arbitrary")),
        name="moba_attn" if use_sel else "diff_attn",
    )(*args)


def _sattn_kernel(pt_ref, *refs, use_sel, nblk, bps, n_new, lam_init):
    del pt_ref
    npg = 2 * bps
    qbd_ref = refs[0]
    kt_refs = refs[1:1 + npg]
    v_refs = refs[1 + npg:1 + 2 * npg]
    kn_ref, vn_ref, om_ref, dl_ref, sg_ref, o_ref, opart, g_scr, m_scr, l_scr = refs[1 + 2 * npg:]
    j = pl.program_id(1)
    qf = qbd_ref[0]
    rows, wdt = qf.shape
    q_hi = qf.astype(BF16)
    q_lo = (qf - q_hi.astype(F32)).astype(BF16)
    lhs = jnp.concatenate([q_hi, q_lo], axis=0)
    lane = lax.broadcasted_iota(jnp.int32, (rows, LANES), 1)
    om = om_ref[...]

    @pl.when(j == 0)
    def _():
        g_scr[...] = jnp.full((rows, LANES), NEG, F32)
        m_scr[...] = jnp.full((rows, LANES), NEG, F32)
        l_scr[...] = jnp.zeros((rows, LANES), F32)

    def scores(kt):
        s2 = jnp.dot(lhs, kt, preferred_element_type=F32)
        return s2[:rows] + s2[rows:]

    gacc, macc, lacc = g_scr[...], m_scr[...], l_scr[...]
    ss = [jnp.concatenate([scores(kt_refs[2 * bb + w][...].astype(BF16)) for w in range(2)], axis=1)
          for bb in range(bps)]
    stats = []
    for bb in range(bps):
        mj = jnp.max(ss[bb], axis=-1, keepdims=True)
        p = jnp.exp(ss[bb] - mj)
        stats.append((mj, jnp.sum(p, axis=-1, keepdims=True), p.astype(BF16)))
    for bb in range(bps):
        blk_id = j * bps + bb
        s = ss[bb]
        mj, lj, pb = stats[bb]
        oj = None
        for w in range(2):
            pw = pb[:, w * LANES:(w + 1) * LANES]
            vref = v_refs[2 * bb + w]
            if use_sel:
                ow = lax.dot_general(pw, vref[...].astype(BF16), NT_DIMS, preferred_element_type=F32)
            else:
                nh = vref.shape[0] // LANES
                ow = jnp.concatenate(
                    [jnp.dot(pw, vref[pl.ds(hh, LANES, stride=nh), :].astype(BF16),
                             preferred_element_type=F32) for hh in range(nh)], axis=1)
            oj = ow if oj is None else oj + ow
        opart[blk_id] = oj * om
        hit = lane == blk_id
        if use_sel:
            gacc = jnp.where(hit, jnp.sum(s, axis=-1, keepdims=True), gacc)
        macc = jnp.where(hit, mj, macc)
        lacc = jnp.where(hit, lj, lacc)
    g_scr[...] = gacc
    m_scr[...] = macc
    l_scr[...] = lacc

    @pl.when(j == nblk // bps - 1)
    def _():
        zpad = jnp.zeros((LANES - NEW_ROWS, wdt), BF16)
        kn = jnp.concatenate([kn_ref[0].astype(BF16), zpad], axis=0)
        vn = jnp.concatenate([vn_ref[0].astype(BF16), zpad], axis=0)
        so2 = lax.dot_general(lhs, kn, NT_DIMS, preferred_element_type=F32)
        so = so2[:rows] + so2[rows:]
        rtok = lax.broadcasted_iota(jnp.int32, (rows, LANES), 0) // N_UNITS
        so = jnp.where((lane <= rtok) & (lane < n_new), so, NEG)
        mo = jnp.max(so, axis=-1, keepdims=True)
        po = jnp.exp(so - mo)
        lo = jnp.sum(po, axis=-1, keepdims=True)
        oo = jnp.dot(po.astype(BF16), vn, preferred_element_type=F32) * om

        inb = lane < nblk
        if use_sel:
            g = jnp.where(inb, gacc, -jnp.inf)
            selm = jnp.zeros((rows, LANES), F32)
            for _ in range(min(MOBA_TOPK, nblk)):
                top = jnp.max(g, axis=-1, keepdims=True)
                first = jnp.min(jnp.where(g == top, lane, LANES), axis=-1, keepdims=True)
                hit = lane == first
                selm = jnp.where(hit, 1.0, selm)
                g = jnp.where(hit, -jnp.inf, g)
            selb = selm > 0.5
        else:
            selb = inb
        mt = jnp.maximum(jnp.max(jnp.where(selb, macc, NEG), axis=-1, keepdims=True), mo)
        w = jnp.where(selb, jnp.exp(macc - mt), 0.0)
        wo = jnp.exp(mo - mt)
        ltot = jnp.sum(w * lacc, axis=-1, keepdims=True) + wo * lo
        acc = wo * oo
        for jb in range(nblk):
            acc = acc + w[:, jb:jb + 1] * opart[jb]
        res = acc / ltot
        if not use_sel:
            lam = _diff_lambda(dl_ref, lam_init)
            rr = lax.broadcasted_iota(jnp.int32, (rows, 1), 0)
            res = res * jnp.where(rr % 2 == 0, 1.0, -lam)
        out = jnp.sum(res.reshape(rows // N_UNITS, N_UNITS, wdt), axis=1)
        if not use_sel:
            dv = sg_ref.shape[-1]
            segs = []
            for hh in range(wdt // dv):
                seg = out[:, hh * dv:(hh + 1) * dv]
                ms = jnp.mean(seg * seg, axis=-1, keepdims=True)
                segs.append(seg * lax.rsqrt(ms + EPS) * sg_ref[...] * (1.0 - lam_init))
            out = jnp.concatenate(segs, axis=1)
        o_ref[0] = out


def sattn_call(page_table, qbd, cache_kt, cache_v, knew, vnew, omask, dlam, subg, layer, use_sel,
               lam_init, n_new):
    db, rows, w = qbd.shape
    page = cache_kt.shape[3]
    n_pages = page_table.shape[1]
    nblk = n_pages * page // ATT_BLOCK
    bps = math.gcd(nblk, SAMPLE_BLOCKS_PER_STEP)
    assert ATT_BLOCK == 2 * page and page == LANES and nblk <= LANES and nblk % bps == 0
    pt_flat = page_table.reshape(-1)

    def page_spec(arr, which):
        return pl.BlockSpec((None, None) + arr.shape[2:],
                            lambda b, j, pt: (layer, pt[b * n_pages + 2 * bps * j + which], 0, 0))

    kern = functools.partial(_sattn_kernel, use_sel=use_sel, nblk=nblk, bps=bps, n_new=n_new,
                             lam_init=lam_init)
    n_tok = rows // N_UNITS
    grid_spec = pltpu.PrefetchScalarGridSpec(
        num_scalar_prefetch=1,
        grid=(db, nblk // bps),
        in_specs=(
            [pl.BlockSpec((1, rows, w), lambda b, j, pt: (b, 0, 0))]
            + [page_spec(cache_kt, i) for i in range(2 * bps)]
            + [page_spec(cache_v, i) for i in range(2 * bps)]
            + [pl.BlockSpec((1, NEW_ROWS, w), lambda b, j, pt: (b, 0, 0)),
               pl.BlockSpec((1, NEW_ROWS, w), lambda b, j, pt: (b, 0, 0)),
               pl.BlockSpec((rows, w), lambda b, j, pt: (0, 0)),
               pl.BlockSpec(dlam.shape, lambda b, j, pt: (0, 0)),
               pl.BlockSpec(subg.shape, lambda b, j, pt: (0, 0))]
        ),
        out_specs=pl.BlockSpec((1, n_tok, w), lambda b, j, pt: (b, 0, 0)),
        scratch_shapes=[pltpu.VMEM((nblk, rows, w), F32), pltpu.VMEM((rows, LANES), F32),
                        pltpu.VMEM((rows, LANES), F32), pltpu.VMEM((rows, LANES), F32)],
    )
    return pl.pallas_call(
        kern,
        grid_spec=grid_spec,
        out_shape=jax.ShapeDtypeStruct((db, n_tok, w), F32),
        compiler_params=_cparams(("arbitrary", "arbitrary")),
        name="moba_sample" if use_sel else "diff_sample",
    )(pt_flat, qbd, *([cache_kt] * (2 * bps)), *([cache_v] * (2 * bps)), knew, vnew, omask, dlam, subg)


def _merge_kernel(x_ref, u_ref, v_ref, ob_ref, oc_ref, ga_ref, gb_ref, gc_ref, gt1_ref, sc2_ref,
                  sh2_ref, n2_ref, ws_ref, bs_ref, wbr_ref, wo_ref, wr_ref, br_ref,
                  x1_ref, h2_ref, rt_ref, cnt_ref, *, o_transposed):
    tm = x_ref.shape[0]
    gd = ws_ref.shape[-1]
    v = v_ref[...].astype(BF16)
    chunks = []
    for c in range(tm // GM_CHUNK):
        r0 = c * GM_CHUNK
        cols = [jnp.dot(ws_ref[g], v[r0:r0 + GM_CHUNK, g * gd:(g + 1) * gd], preferred_element_type=F32)
                for g in range(GM_GROUPS)]
        chunks.append(jnp.concatenate(cols, axis=1) + bs_ref[...])
    mixed = jnp.concatenate(chunks, axis=0) if len(chunks) > 1 else chunks[0]
    o_a = u_ref[...] * mixed
    if o_transposed:
        o_b = ob_ref[...].T
        o_c = oc_ref[...].T
    else:
        o_b = ob_ref[...]
        o_c = oc_ref[...]
    merged = (ga_ref[...] * jnp.dot(o_a.astype(BF16), wbr_ref[0], preferred_element_type=F32)
              + gb_ref[...] * jnp.dot(o_b.astype(BF16), wbr_ref[1], preferred_element_type=F32)
              + gc_ref[...] * jnp.dot(o_c.astype(BF16), wbr_ref[2], preferred_element_type=F32))
    y = jnp.dot(merged.astype(BF16), wo_ref[...], preferred_element_type=F32)
    x1 = x_ref[...] + gt1_ref[0] * y
    x1_ref[...] = x1
    h2 = x1 * lax.rsqrt(jnp.mean(x1 * x1, axis=-1, keepdims=True) + EPS) * n2_ref[...]
    h2 = h2 * (1.0 + sc2_ref[0]) + sh2_ref[0]
    h2_ref[...] = h2

    h_hi = h2.astype(BF16)
    h_lo = (h2 - h_hi.astype(F32)).astype(BF16)
    logits = (jnp.dot(h_hi, wr_ref[0], preferred_element_type=F32)
              + jnp.dot(h_lo, wr_ref[0], preferred_element_type=F32)
              + jnp.dot(h_hi, wr_ref[1], preferred_element_type=F32)) + br_ref[...]
    lane = lax.broadcasted_iota(jnp.int32, logits.shape, 1)
    gl = jnp.where(lane < N_GROUPS, logits, NEG)
    gmax = jnp.max(gl, axis=-1, keepdims=True)
    gsel = jnp.min(jnp.where(gl == gmax, lane, LANES), axis=-1, keepdims=True)
    gw = 1.0 / jnp.sum(jnp.exp(gl - gmax), axis=-1, keepdims=True)
    lo = N_GROUPS + EXPERTS_PER_GROUP * gsel
    el = jnp.where((lane >= lo) & (lane < lo + EXPERTS_PER_GROUP), logits, NEG)
    v1 = jnp.max(el, axis=-1, keepdims=True)
    i1 = jnp.min(jnp.where(el == v1, lane, LANES), axis=-1, keepdims=True)
    el2 = jnp.where(lane == i1, NEG, el)
    v2 = jnp.max(el2, axis=-1, keepdims=True)
    i2 = jnp.min(jnp.where(el2 == v2, lane, LANES), axis=-1, keepdims=True)
    e2 = jnp.exp(v2 - v1)
    w1 = gw / (1.0 + e2)
    w2 = gw * e2 / (1.0 + e2)

    @pl.when(pl.program_id(0) == 0)
    def _():
        cnt_ref[...] = jnp.zeros(cnt_ref.shape, F32)

    e1 = i1 - N_GROUPS
    e2i = i2 - N_GROUPS
    onehot = jnp.where((lane == e1) | (lane == e2i), 1.0, 0.0)
    rr = lax.broadcasted_iota(jnp.int32, (tm, tm), 0)
    cc = lax.broadcasted_iota(jnp.int32, (tm, tm), 1)
    lower = jnp.where(rr > cc, 1.0, 0.0).astype(BF16)
    before = jnp.dot(lower, onehot.astype(BF16), preferred_element_type=F32) + cnt_ref[0:1, :]
    r1 = jnp.sum(jnp.where(lane == e1, before, 0.0), axis=-1, keepdims=True)
    r2 = jnp.sum(jnp.where(lane == e2i, before, 0.0), axis=-1, keepdims=True)
    cnt_ref[...] = cnt_ref[...] + jnp.sum(onehot, axis=0, keepdims=True)

    vals = (e1.astype(F32), e2i.astype(F32), w1, w2, r1, r2)
    rt = jnp.zeros(logits.shape, F32)
    for k, val in enumerate(vals):
        rt = jnp.where(lane == k, val, rt)
    rt_ref[...] = rt


def merge_call(x, uv, gates, ob, oc, gt1, sc2, sh2, n2, ws_bf, bs_tab, wbr_bf, wo_bf, wr, br, tm,
               seq_tiles, uv_col0, gate_col0, o_transposed):
    t, d = x.shape
    bw = d // 2
    mod_rows = gt1.shape[1]
    if mod_rows == 1:
        mod_map = lambda i: (i // seq_tiles, 0, 0)
    else:
        mod_map = lambda i: (0, 0, 0)
    if o_transposed:
        o_spec = pl.BlockSpec((None, bw, tm), lambda i: (i // seq_tiles, 0, i % seq_tiles))
    else:
        o_spec = pl.BlockSpec((tm, bw), lambda i: (i, 0))
    full = lambda a: pl.BlockSpec(a.shape, lambda i: (0,) * a.ndim)
    return pl.pallas_call(
        functools.partial(_merge_kernel, o_transposed=o_transposed),
        grid=(t // tm,),
        in_specs=[
            pl.BlockSpec((tm, d), lambda i: (i, 0)),
            pl.BlockSpec((tm, bw), lambda i: (i, uv_col0)),
            pl.BlockSpec((tm, bw), lambda i: (i, uv_col0 + 1)),
            o_spec, o_spec,
            pl.BlockSpec((tm, d), lambda i: (i, gate_col0)),
            pl.BlockSpec((tm, d), lambda i: (i, gate_col0 + 1)),
            pl.BlockSpec((tm, d), lambda i: (i, gate_col0 + 2)),
            pl.BlockSpec((1, mod_rows, d), mod_map),
            pl.BlockSpec((1, mod_rows, d), mod_map),
            pl.BlockSpec((1, mod_rows, d), mod_map),
            full(n2), full(ws_bf), full(bs_tab), full(wbr_bf), full(wo_bf), full(wr), full(br),
        ],
        out_specs=[
            pl.BlockSpec((tm, d), lambda i: (i, 0)),
            pl.BlockSpec((tm, d), lambda i: (i, 0)),
            pl.BlockSpec((tm, LANES), lambda i: (i, 0)),
            pl.BlockSpec((8, LANES), lambda i: (0, 0)),
        ],
        out_shape=[jax.ShapeDtypeStruct((t, d), F32), jax.ShapeDtypeStruct((t, d), F32),
                   jax.ShapeDtypeStruct((t, LANES), F32), jax.ShapeDtypeStruct((8, LANES), F32)],
        compiler_params=_cparams(("arbitrary",)),
        name="merge",
    )(x, uv, uv, ob, oc, gates, gates, gates, gt1, sc2, sh2, n2, ws_bf, bs_tab, wbr_bf, wo_bf, wr, br)


def _gather_kernel(idx_ref, src_ref, o_ref, sem):
    rows = o_ref.shape[0]
    base = pl.program_id(0) * rows

    def row_copy(r, tok):
        return pltpu.make_async_copy(src_ref.at[pl.ds(tok, 1)], o_ref.at[pl.ds(r, 1)], sem)

    def issue(g, carry):
        for k in range(GATHER_UNROLL):
            r = g * GATHER_UNROLL + k
            row_copy(r, idx_ref[base + r]).start(priority=k % 2)
        return carry

    lax.fori_loop(0, rows // GATHER_UNROLL, issue, 0)

    def drain(r, carry):
        row_copy(r, 0).wait()
        return carry

    lax.fori_loop(0, rows, drain, 0, unroll=8)


def gather_call(idx, src, rows_per_step):
    n = idx.shape[0]
    d = src.shape[1]
    grid_spec = pltpu.PrefetchScalarGridSpec(
        num_scalar_prefetch=1,
        grid=(n // rows_per_step,),
        in_specs=[pl.BlockSpec(memory_space=pl.ANY)],
        out_specs=pl.BlockSpec((rows_per_step, d), lambda i, idx: (i, 0)),
        scratch_shapes=[pltpu.SemaphoreType.DMA(())],
    )
    return pl.pallas_call(
        _gather_kernel,
        grid_spec=grid_spec,
        out_shape=jax.ShapeDtypeStruct((n, d), src.dtype),
        compiler_params=_cparams(("arbitrary",)),
        name="row_gather",
    )(idx, src)


def _scatter_kernel(dest_ref, h_ref, init_ref, o_ref, sem):
    del init_ref
    tm = h_ref.shape[0]
    base = pl.program_id(0) * tm
    t_total = pl.num_programs(0) * tm

    def row_copy(r, dst):
        return pltpu.make_async_copy(h_ref.at[pl.ds(r, 1)], o_ref.at[pl.ds(dst, 1)], sem)

    def issue(g, carry):
        for k in range(GATHER_UNROLL):
            r = g * GATHER_UNROLL + k
            for slot in range(EXPERT_TOPK):
                row_copy(r, dest_ref[slot * t_total + base + r]).start(priority=slot)
        return carry

    lax.fori_loop(0, tm // GATHER_UNROLL, issue, 0)

    def drain(r, carry):
        row_copy(0, 0).wait()
        return carry

    lax.fori_loop(0, tm * EXPERT_TOPK, drain, 0, unroll=8)


def scatter_call(dest, h2, init, tm):
    t, d = h2.shape
    grid_spec = pltpu.PrefetchScalarGridSpec(
        num_scalar_prefetch=1,
        grid=(t // tm,),
        in_specs=[pl.BlockSpec((tm, d), lambda i, dest: (i, 0)), pl.BlockSpec(memory_space=pl.ANY)],
        out_specs=pl.BlockSpec(memory_space=pl.ANY),
        scratch_shapes=[pltpu.SemaphoreType.DMA(())],
    )
    return pl.pallas_call(
        _scatter_kernel,
        grid_spec=grid_spec,
        out_shape=jax.ShapeDtypeStruct(init.shape, init.dtype),
        input_output_aliases={2: 0},
        compiler_params=_cparams(("arbitrary",)),
        name="row_scatter",
    )(dest, h2, init)


def _expert_kernel(be_ref, nu_ref, x_ref, wg_ref, wu_ref, wd_ref, o_ref, wg_bf, wu_bf, wd_bf):
    i = pl.program_id(0)
    e = be_ref[i]
    prev = be_ref[jnp.maximum(i - 1, 0)]

    @pl.when((i == 0) | (e != prev))
    def _():
        wg_bf[...] = wg_ref[...].astype(BF16)
        wu_bf[...] = wu_ref[...].astype(BF16)
        wd_bf[...] = wd_ref[...].astype(BF16)

    @pl.when(i < nu_ref[0])
    def _():
        x = x_ref[...].astype(BF16)
        g = jnp.dot(x, wg_bf[...], preferred_element_type=F32)
        u = jnp.dot(x, wu_bf[...], preferred_element_type=F32)
        h = g * jax.nn.sigmoid(g) * u
        o_ref[...] = jnp.dot(h.astype(BF16), wd_bf[...], preferred_element_type=F32)

    @pl.when(i >= nu_ref[0])
    def _():
        o_ref[...] = jnp.zeros(o_ref.shape, o_ref.dtype)


def expert_call(blk_e, n_used, xs, wg, wu, wd, layer, rows):
    n, d = xs.shape
    f = wg.shape[-1]
    nblk = n // rows
    grid_spec = pltpu.PrefetchScalarGridSpec(
        num_scalar_prefetch=2,
        grid=(nblk,),
        in_specs=[
            pl.BlockSpec((rows, d), lambda i, be, nu: (i, 0)),
            pl.BlockSpec((None, None, d, f), lambda i, be, nu: (layer, be[i], 0, 0)),
            pl.BlockSpec((None, None, d, f), lambda i, be, nu: (layer, be[i], 0, 0)),
            pl.BlockSpec((None, None, f, d), lambda i, be, nu: (layer, be[i], 0, 0)),
        ],
        out_specs=pl.BlockSpec((rows, d), lambda i, be, nu: (i, 0)),
        scratch_shapes=[pltpu.VMEM((d, f), BF16), pltpu.VMEM((d, f), BF16), pltpu.VMEM((f, d), BF16)],
    )
    return pl.pallas_call(
        _expert_kernel,
        grid_spec=grid_spec,
        out_shape=jax.ShapeDtypeStruct((n, d), F32),
        compiler_params=_cparams(("arbitrary",)),
        name="experts",
    )(blk_e, n_used, xs, wg, wu, wd)


def _combine_kernel(x1_ref, y0_ref, y1_ref, rt_ref, gt2_ref, fg_ref, o_ref, *, final):
    rt = rt_ref[...]
    x2 = x1_ref[...] + gt2_ref[0] * (rt[:, 2:3] * y0_ref[...] + rt[:, 3:4] * y1_ref[...])
    if final:
        x2 = x2 * lax.rsqrt(jnp.mean(x2 * x2, axis=-1, keepdims=True) + EPS) * fg_ref[...]
    o_ref[...] = x2


def combine_call(x1, ya, rt, gt2, fg, tm, seq_tiles, final):
    t, d = x1.shape
    mod_rows = gt2.shape[1]
    if mod_rows == 1:
        mod_map = lambda i: (i // seq_tiles, 0, 0)
    else:
        mod_map = lambda i: (0, 0, 0)
    nt = t // tm
    return pl.pallas_call(
        functools.partial(_combine_kernel, final=final),
        grid=(nt,),
        in_specs=[
            pl.BlockSpec((tm, d), lambda i: (i, 0)),
            pl.BlockSpec((tm, d), lambda i: (i, 0)),
            pl.BlockSpec((tm, d), lambda i: (i + nt, 0)),
            pl.BlockSpec((tm, LANES), lambda i: (i, 0)),
            pl.BlockSpec((1, mod_rows, d), mod_map),
            pl.BlockSpec((1, d), lambda i: (0, 0)),
        ],
        out_specs=pl.BlockSpec((tm, d), lambda i: (i, 0)),
        out_shape=jax.ShapeDtypeStruct((t, d), F32),
        compiler_params=_cparams(("arbitrary",)),
        name="combine",
    )(x1, ya, ya, rt, gt2, fg)


def moe_block(h2, x1, rt, cnt, gt2, fg, wg, wu, wd, layer, tm, seq_tiles, final):
    t, d = h2.shape
    a = t * EXPERT_TOPK
    eid = rt[:, :EXPERT_TOPK].astype(jnp.int32).T
    rank = rt[:, 4:4 + EXPERT_TOPK].astype(jnp.int32).T
    counts = cnt[0, :N_EXPERTS].astype(jnp.int32)
    br = MOE_ROWS if a >= MOE_ROWS * N_EXPERTS else MOE_ROWS_SMALL
    padded = (counts + br - 1) // br * br
    pad_ends = jnp.cumsum(padded)
    pad_starts = pad_ends - padded
    experts = jnp.arange(N_EXPERTS, dtype=jnp.int32)
    start_of = jnp.sum(jnp.where(eid[..., None] == experts, pad_starts, 0), axis=-1)
    dest = (start_of + rank).astype(jnp.int32).reshape(-1)
    nblk = -(-a // br) + N_EXPERTS
    n_used = (pad_ends[-1] // br).astype(jnp.int32)
    blk_start = jnp.arange(nblk, dtype=jnp.int32) * br
    blk_e = jnp.sum((pad_ends[None, :] <= blk_start[:, None]).astype(jnp.int32), axis=1)
    blk_e = jnp.minimum(blk_e, N_EXPERTS - 1)
    last_e = blk_e[jnp.maximum(n_used - 1, 0)]
    blk_e = jnp.where(jnp.arange(nblk) < n_used, blk_e, last_e)

    xs = scatter_call(dest, h2, jnp.zeros((nblk * br, d), h2.dtype), tm)
    yb = expert_call(blk_e, n_used.reshape(1), xs, wg, wu, wd, layer, br)
    rows = min(MOE_ROWS, a)
    ya = gather_call(dest, yb, rows)
    return combine_call(x1, ya, rt, gt2, fg, tm, seq_tiles, final)


def layer_prompt(x, mods, lw, layer, rope, b, s, lam_init, final, fg):
    t, d = x.shape
    bw = d // 2
    sh1, sc1, gt1, sh2, sc2, gt2 = mods
    tm_in = min(512, s)
    (uv, gates, mk, mv, dk, dv, mqt, dqt, mkb, dkb, mvt, dvt) = inproj_attn_call(
        x, sc1, sh1, lw['n1'], lw['win'], rope, lw['lng'], lw['lnb'], tm_in, b, s)
    ob_t = attn_call(mqt, mkb, mvt, 1, True)
    oc_t = attn_call(dqt, dkb, dvt, 2, False, lw['dlam'], lw['subg_col'], lam_init)
    tm = 512
    x1, h2, rt, cnt = merge_call(x, uv, gates, ob_t.reshape(b, bw, s), oc_t.reshape(b, bw, s), gt1, sc2, sh2,
                            lw['n2'], lw['ws'], lw['bs_tab'], lw['wbr'], lw['wo'], lw['wr'], lw['br'],
                            tm, s // tm, 0, 0, True)
    x2 = moe_block(h2, x1, rt, cnt, gt2, fg, lw['weg'], lw['weu'], lw['wed'], layer, tm, s // tm, final)
    return x2, (mk, mv, dk, dv)


def layer_sample(x, mods, lw, layer, rope, db, ds, lam_init, final, fg, caches, page_table):
    t, d = x.shape
    bw = d // 2
    sh1, sc1, gt1, sh2, sc2, gt2 = mods
    zt = inproj_call(x, sc1, sh1, lw['n1'], lw['win'], rope, lw['lng'], lw['lnb'], t, 1)
    scale = HEAD_DIM ** -0.5
    eye = jnp.eye(N_UNITS, dtype=F32)

    def qbd(col0):
        q = zt[:, col0:col0 + bw].reshape(db, ds, N_UNITS, HEAD_DIM) * scale
        return jnp.einsum('btud,uw->btuwd', q, eye).reshape(db, ds * N_UNITS, bw)

    def pad_new(col0):
        a = zt[:, col0:col0 + bw].reshape(db, ds, bw)
        return jnp.pad(a, ((0, 0), (0, NEW_ROWS - ds), (0, 0)))

    unit = np.arange(ds * N_UNITS) % N_UNITS
    col = np.arange(bw)
    om_moba = jnp.asarray((col[None, :] // HEAD_DIM == unit[:, None]).astype(np.float32))
    om_diff = jnp.asarray((col[None, :] // (2 * HEAD_DIM) == unit[:, None] // 2).astype(np.float32))
    ckt, cvt, cdkt, cdv = caches
    ob = sattn_call(page_table, qbd(2 * bw), ckt, cvt, pad_new(3 * bw), pad_new(4 * bw), om_moba,
                    lw['dlam'], lw['subg_row'], layer, True, lam_init, ds).reshape(t, bw)
    oc = sattn_call(page_table, qbd(5 * bw), cdkt, cdv, pad_new(6 * bw), pad_new(7 * bw), om_diff,
                    lw['dlam'], lw['subg_row'], layer, False, lam_init, ds).reshape(t, bw)
    x1, h2, rt, cnt = merge_call(x, zt, zt, ob, oc, gt1, sc2, sh2, lw['n2'], lw['ws_s'], lw['bs_tab_s'], lw['wbr'],
                            lw['wo'], lw['wr'], lw['br'], t, 1, 0, (8 * bw) // d, False)
    x2 = moe_block(h2, x1, rt, cnt, gt2, fg, lw['weg'], lw['weu'], lw['wed'], layer, t, 1, final)
    outs = (zt[:, 3 * bw:4 * bw], zt[:, 4 * bw:5 * bw], zt[:, 6 * bw:7 * bw], zt[:, 7 * bw:8 * bw],
            zt[:, bw:2 * bw])
    return x2, outs


def kernel(x_prompt, x_sample, cache_moba_k, cache_moba_v, cache_diff_k, cache_diff_v, page_table, c_prompt, c_sample, norm1_g, norm2_g, final_g, w_ada, b_ada, w_in, gm_ln_g, gm_ln_b, gm_ws, gm_bs, diff_lambda, diff_subln_g, w_branch, w_out, w_group, b_group, w_router, b_router, w_gate_e, w_up_e, w_down_e):
    b, s, d = x_prompt.shape
    db, ds, _ = x_sample.shape
    depth = w_in.shape[0]
    bw = d // 2
    gd = bw // GM_GROUPS
    npool, page = cache_moba_k.shape[1], cache_moba_k.shape[2]
    past_len = page_table.shape[1] * page
    ts = db * ds
    assert s % ATT_BLOCK == 0 and ts % GM_CHUNK == 0 and GM_CHUNK % ds == 0

    rope_p = rope_tables(jnp.arange(s, dtype=jnp.int32))
    rope_s = rope_tables(jnp.tile(past_len + jnp.arange(ds, dtype=jnp.int32), db))
    caches = (
        jnp.transpose(cache_moba_k, (0, 1, 3, 4, 2)).reshape(depth, npool, bw, page),
        jnp.transpose(cache_moba_v, (0, 1, 3, 4, 2)).reshape(depth, npool, bw, page),
        jnp.transpose(cache_diff_k, (0, 1, 3, 4, 5, 2)).reshape(depth, npool, bw, page),
        cache_diff_v.reshape(depth, npool, page * cache_diff_v.shape[3], cache_diff_v.shape[4]),
    )

    nc = b + db
    ncp = -(-nc // 8) * 8
    c_all = jnp.pad(jnp.concatenate([c_prompt, c_sample], axis=0), ((0, ncp - nc), (0, 0)))
    tril = jnp.tril(jnp.ones((GM_CHUNK, GM_CHUNK), F32))
    fg = final_g.reshape(1, d)

    xp = x_prompt.reshape(b * s, d)
    xs = x_sample.reshape(ts, d)
    outs_p, outs_s = [], []
    for l in range(depth):
        lam_init = 0.8 - 0.6 * math.exp(-0.3 * l)
        final = l == depth - 1
        m = ada_call(c_all, w_ada, b_ada, l)
        parts = [m[:, i * d:(i + 1) * d] for i in range(6)]
        mods_p = tuple(p[:b].reshape(b, 1, d) for p in parts)
        mods_s = tuple(jnp.repeat(p[b:nc], ds, axis=0).reshape(1, ts, d) for p in parts)

        ws_tril = gm_ws[l] * tril[None]
        ws_small = ws_tril[:, :ds, :ds]
        ws_s = jnp.einsum('ab,gts->gatbs', jnp.eye(GM_CHUNK // ds, dtype=F32), ws_small)
        ws_s = ws_s.reshape(GM_GROUPS, GM_CHUNK, GM_CHUNK)
        bs_tab = jnp.repeat(gm_bs[l].T, gd, axis=1)
        bs_tab_s = jnp.tile(bs_tab[:ds], (GM_CHUNK // ds, 1))
        wr = jnp.pad(jnp.concatenate([w_group[l], w_router[l]], axis=1),
                     ((0, 0), (0, LANES - N_GROUPS - N_EXPERTS)))
        wr_hi = wr.astype(BF16)
        wr = jnp.stack([wr_hi, (wr - wr_hi.astype(F32)).astype(BF16)])
        br = jnp.pad(jnp.concatenate([b_group[l], b_router[l]]), (0, LANES - N_GROUPS - N_EXPERTS))
        lw = dict(
            n1=norm1_g[l].reshape(1, d), n2=norm2_g[l].reshape(1, d),
            win=w_in[l].astype(BF16), lng=gm_ln_g[l].reshape(1, bw), lnb=gm_ln_b[l].reshape(1, bw),
            ws=ws_tril.astype(BF16), bs_tab=bs_tab, ws_s=ws_s.astype(BF16), bs_tab_s=bs_tab_s,
            dlam=diff_lambda[l], subg_col=diff_subln_g[l].reshape(-1, 1), subg_row=diff_subln_g[l].reshape(1, -1),
            wbr=w_branch[l].astype(BF16), wo=w_out[l].astype(BF16), wr=wr, br=br.reshape(1, LANES),
            weg=w_gate_e, weu=w_up_e, wed=w_down_e,
        )
        xp, op = layer_prompt(xp, mods_p, lw, l, rope_p, b, s, lam_init, final, fg)
        xs, os_ = layer_sample(xs, mods_s, lw, l, rope_s, db, ds, lam_init, final, fg, caches, page_table)
        outs_p.append(op)
        outs_s.append(os_)

    nh = N_UNITS
    stack = lambda lst, i, shape: jnp.stack([o[i].reshape(shape) for o in lst])
    tok_last = lambda i, shape: jnp.moveaxis(stack(outs_p, i, (b,) + shape + (s,)), -1, 2)
    return (
        xp.reshape(b, s, d), xs.reshape(db, ds, d),
        tok_last(0, (nh, HEAD_DIM)), tok_last(1, (nh, HEAD_DIM)),
        tok_last(2, (nh // 2, 2, HEAD_DIM)), stack(outs_p, 3, (b, s, nh // 2, 2 * HEAD_DIM)),
        stack(outs_s, 0, (db, ds, nh, HEAD_DIM)), stack(outs_s, 1, (db, ds, nh, HEAD_DIM)),
        stack(outs_s, 2, (db, ds, nh // 2, 2, HEAD_DIM)), stack(outs_s, 3, (db, ds, nh // 2, 2 * HEAD_DIM)),
        stack(outs_s, 4, (db, ds, bw)),
    )
```
